```python
import math
import jax, jax.numpy as jnp
from jax import lax
import numpy as np

D_MODEL = 1024
BATCH = 16
SEQ = 4096
DEPTH = 4

A_WIDTH = D_MODEL
A_GROUPS = 8
A_GROUP_DIM = A_WIDTH // A_GROUPS
A_CHUNK = 128
B_WIDTH = D_MODEL
B_HEADS = 4
B_HEAD_DIM = B_WIDTH // B_HEADS
B_CONV = 4
MLSTM_CHUNK = 128
EVEN_IN = 3 * A_WIDTH + 3 * B_WIDTH + 2 * B_HEADS
EVEN_OUT = A_WIDTH + B_WIDTH
EVEN_SPLITS = (A_WIDTH, 2 * A_WIDTH, 3 * A_WIDTH, 3 * A_WIDTH + B_WIDTH,
               3 * A_WIDTH + 2 * B_WIDTH, 3 * A_WIDTH + 2 * B_WIDTH + B_HEADS,
               3 * A_WIDTH + 2 * B_WIDTH + 2 * B_HEADS)
C_WIDTH = 2 * D_MODEL
C_HEADS = 16
C_HEAD_DIM = C_WIDTH // (2 * C_HEADS)
ODD_IN = 4 * C_WIDTH
Q_BLOCK = 128
ROPE_THETA = 10000.0
NORM_EPS = 1e-6

kernel_name = "hybrid_gmlp_mlstm_diffattn"


def rmsnorm(x, g):
    xf = x.astype(jnp.float32)
    y = xf * lax.rsqrt(jnp.mean(xf * xf, axis=-1, keepdims=True) + NORM_EPS)
    return (y * g.astype(jnp.float32)).astype(x.dtype)


def layernorm(x, g):
    xf = x.astype(jnp.float32)
    xc = xf - jnp.mean(xf, axis=-1, keepdims=True)
    y = xc * lax.rsqrt(jnp.mean(xc * xc, axis=-1, keepdims=True) + NORM_EPS)
    return (y * g.astype(jnp.float32)).astype(x.dtype)


def causal_depthwise_conv(x, w, b):
    k = w.shape[0]
    y = lax.conv_general_dilated(x, w[:, None, :].astype(x.dtype), window_strides=(1,),
                                 padding=((k - 1, 0),), dimension_numbers=("NWC", "WIO", "NWC"),
                                 feature_group_count=x.shape[-1])
    return y + b


def rope_tables(seq, dim):
    inv = ROPE_THETA ** (-jnp.arange(0, dim, 2, dtype=jnp.float32) / dim)
    ang = jnp.arange(seq, dtype=jnp.float32)[:, None] * inv[None, :]
    return jnp.cos(ang), jnp.sin(ang)


def apply_rope(x, cos, sin):
    shape = (x.shape[1],) + (1,) * (x.ndim - 3) + (x.shape[-1] // 2,)
    c, s = cos.reshape(shape), sin.reshape(shape)
    x1, x2 = jnp.split(x.astype(jnp.float32), 2, axis=-1)
    return jnp.concatenate([x1 * c - x2 * s, x1 * s + x2 * c], axis=-1).astype(x.dtype)


def mlstm_chunkwise(q, k, v, i_pre, f_pre):
    bsz, nh, seq, dk = q.shape
    dv = v.shape[-1]
    L = MLSTM_CHUNK
    nc = seq // L

    def chunks(t):
        t = t.astype(jnp.float32)
        return jnp.moveaxis(t.reshape((bsz, nh, nc, L) + t.shape[3:]), 2, 0)

    logf = jax.nn.log_sigmoid(f_pre.astype(jnp.float32))
    causal = jnp.tril(jnp.ones((L, L), dtype=bool))

    def body(carry, xs):
        C, n, m = carry
        qc, kc, vc, ic, lfc = xs
        b = jnp.cumsum(lfc, axis=-1)
        a = b + m[..., None]
        D = jnp.where(causal, b[..., :, None] - b[..., None, :] + ic[..., None, :], -jnp.inf)
        m_t = jnp.maximum(a, jnp.max(D, axis=-1))
        sc = jnp.einsum('bhtd,bhsd->bhts', qc, kc) * jnp.exp(D - m_t[..., None])
        w_inter = jnp.exp(a - m_t)
        num = jnp.einsum('bhts,bhse->bhte', sc, vc) + w_inter[..., None] * jnp.einsum('bhtd,bhde->bhte', qc, C)
        den = jnp.sum(sc, axis=-1) + w_inter * jnp.einsum('bhtd,bhd->bht', qc, n)
        h = num / jnp.maximum(jnp.abs(den), jnp.exp(-m_t))[..., None]
        bl = b[..., -1]
        g = bl[..., None] - b + ic
        m_new = jnp.maximum(bl + m, jnp.max(g, axis=-1))
        wk = jnp.exp(g - m_new[..., None])
        decay = jnp.exp(bl + m - m_new)
        C_new = decay[..., None, None] * C + jnp.einsum('bhs,bhsd,bhse->bhde', wk, kc, vc)
        n_new = decay[..., None] * n + jnp.einsum('bhs,bhsd->bhd', wk, kc)
        return (C_new, n_new, m_new), h

    init = (jnp.zeros((bsz, nh, dk, dv), jnp.float32), jnp.zeros((bsz, nh, dk), jnp.float32),
            jnp.zeros((bsz, nh), jnp.float32))
    _, hs = lax.scan(body, init, (chunks(q), chunks(k) * dk ** -0.5, chunks(v), chunks(i_pre), chunks(logf)))
    return jnp.moveaxis(hs, 0, 2).reshape(bsz, nh, seq, dv).astype(v.dtype)


def even_layer(h, w_in, w_out, a_ln_g, a_ws, a_bs, b_conv_w, b_conv_b, b_wq, b_wk, b_wv,
               b_ig_b, b_fg_b, b_gn_g, b_skip):
    bsz, seq, _ = h.shape
    u, va, za, xm, og, ig, fg, zb = jnp.split(h @ w_in, EVEN_SPLITS, axis=-1)
    u = jax.nn.gelu(u)
    va = layernorm(jax.nn.gelu(va), a_ln_g)
    vg = va.reshape(bsz, seq // A_CHUNK, A_CHUNK, A_GROUPS, A_GROUP_DIM)
    sg = jnp.einsum('gts,bcsgd->bctgd', jnp.tril(a_ws), vg) + a_bs.T[:, :, None]
    y_a = u * sg.reshape(bsz, seq, A_WIDTH) * jax.nn.silu(za)
    xc = jax.nn.silu(causal_depthwise_conv(xm, b_conv_w, b_conv_b))
    xch = xc.reshape(bsz, seq, B_HEADS, B_HEAD_DIM)
    xmh = xm.reshape(bsz, seq, B_HEADS, B_HEAD_DIM)
    q = jnp.einsum('bshd,hde->bhse', xch, b_wq)
    k = jnp.einsum('bshd,hde->bhse', xch, b_wk)
    v = jnp.einsum('bshd,hde->bhse', xmh, b_wv)
    ht = mlstm_chunkwise(q, k, v, jnp.swapaxes(ig + b_ig_b, 1, 2), jnp.swapaxes(fg + b_fg_b, 1, 2))
    ht = jnp.swapaxes(ht, 1, 2) * jax.nn.sigmoid(og).reshape(bsz, seq, B_HEADS, B_HEAD_DIM)
    ht = layernorm(ht, b_gn_g.reshape(B_HEADS, B_HEAD_DIM))
    y_b = (ht.reshape(bsz, seq, B_WIDTH) + b_skip * xc) * jax.nn.silu(zb)
    return jnp.concatenate([y_a, y_b], axis=-1) @ w_out


def diff_attention(q1, q2, k1, k2, v, lam):
    seq, d = q1.shape[1], q1.shape[-1]
    scale = d ** -0.5
    outs = []
    for j in range(seq // Q_BLOCK):
        lo, hi = j * Q_BLOCK, (j + 1) * Q_BLOCK
        mask = jnp.arange(hi)[None, :] <= jnp.arange(lo, hi)[:, None]

        def probs(qa, ka):
            s = jnp.einsum('bqhd,bkhd->bhqk', qa[:, lo:hi], ka[:, :hi]).astype(jnp.float32) * scale
            return jax.nn.softmax(jnp.where(mask, s, -jnp.inf), axis=-1)

        p = probs(q1, k1) - lam * probs(q2, k2)
        outs.append(jnp.einsum('bhqk,bkhe->bqhe', p.astype(v.dtype), v[:, :hi]))
    return jnp.concatenate(outs, axis=1)


def odd_layer(h, w_in, w_out, lq1, lk1, lq2, lk2, subln_g, lam_init, cos, sin):
    bsz, seq, _ = h.shape
    q, k, v, z = jnp.split(h @ w_in, 4, axis=-1)
    q = apply_rope(q.reshape(bsz, seq, C_HEADS, 2, C_HEAD_DIM), cos, sin)
    k = apply_rope(k.reshape(bsz, seq, C_HEADS, 2, C_HEAD_DIM), cos, sin)
    v = v.reshape(bsz, seq, C_HEADS, 2 * C_HEAD_DIM)
    f32 = jnp.float32
    lam = (jnp.exp(jnp.sum(lq1.astype(f32) * lk1.astype(f32))) -
           jnp.exp(jnp.sum(lq2.astype(f32) * lk2.astype(f32))) + lam_init)
    o = diff_attention(q[..., 0, :], q[..., 1, :], k[..., 0, :], k[..., 1, :], v, lam)
    o = rmsnorm(o, subln_g) * (1.0 - lam_init)
    o = o.reshape(bsz, seq, C_WIDTH) * jax.nn.silu(z)
    return o @ w_out


def lambda_init(layer):
    return 0.8 - 0.6 * math.exp(-0.3 * layer)


def setup_inputs(seed: int = 0) -> dict:
    key = jax.random.key(seed)
    ks = jax.random.split(key, 24)
    ne, no = (DEPTH + 1) // 2, DEPTH // 2

    def nrm(k, shape, scale):
        return jax.random.normal(k, shape, jnp.float32) * scale

    return {
        "x": nrm(ks[0], (BATCH, SEQ, D_MODEL), 1.0),
        "norm_g": 1.0 + nrm(ks[1], (DEPTH, D_MODEL), 0.02),
        "ev_w_in": nrm(ks[2], (ne, D_MODEL, EVEN_IN), D_MODEL ** -0.5),
        "ev_w_out": nrm(ks[3], (ne, EVEN_OUT, D_MODEL), EVEN_OUT ** -0.5),
        "a_ln_g": 1.0 + nrm(ks[4], (ne, A_WIDTH), 0.02),
        "a_ws": nrm(ks[5], (ne, A_GROUPS, A_CHUNK, A_CHUNK), A_CHUNK ** -0.5),
        "a_bs": 1.0 + nrm(ks[6], (ne, A_GROUPS, A_CHUNK), 0.1),
        "b_conv_w": nrm(ks[7], (ne, B_CONV, B_WIDTH), B_CONV ** -0.5),
        "b_conv_b": nrm(ks[8], (ne, B_WIDTH), 0.02),
        "b_wq": nrm(ks[9], (ne, B_HEADS, B_HEAD_DIM, B_HEAD_DIM), B_HEAD_DIM ** -0.5),
        "b_wk": nrm(ks[10], (ne, B_HEADS, B_HEAD_DIM, B_HEAD_DIM), B_HEAD_DIM ** -0.5),
        "b_wv": nrm(ks[11], (ne, B_HEADS, B_HEAD_DIM, B_HEAD_DIM), B_HEAD_DIM ** -0.5),
        "b_ig_b": nrm(ks[12], (ne, B_HEADS), 0.1),
        "b_fg_b": jnp.linspace(3.0, 6.0, B_HEADS, dtype=jnp.float32) + nrm(ks[13], (ne, B_HEADS), 0.1),
        "b_gn_g": 1.0 + nrm(ks[14], (ne, B_WIDTH), 0.02),
        "b_skip": 1.0 + nrm(ks[15], (ne, B_WIDTH), 0.02),
        "od_w_in": nrm(ks[16], (no, D_MODEL, ODD_IN), D_MODEL ** -0.5),
        "od_w_out": nrm(ks[17], (no, C_WIDTH, D_MODEL), C_WIDTH ** -0.5),
        "c_lam_q1": nrm(ks[18], (no, C_HEAD_DIM), 0.1),
        "c_lam_k1": nrm(ks[19], (no, C_HEAD_DIM), 0.1),
        "c_lam_q2": nrm(ks[20], (no, C_HEAD_DIM), 0.1),
        "c_lam_k2": nrm(ks[21], (no, C_HEAD_DIM), 0.1),
        "c_subln_g": 1.0 + nrm(ks[22], (no, 2 * C_HEAD_DIM), 0.02),
        "final_g": 1.0 + nrm(ks[23], (D_MODEL,), 0.02),
    }


def reference(x, norm_g, ev_w_in, ev_w_out, a_ln_g, a_ws, a_bs, b_conv_w, b_conv_b, b_wq, b_wk, b_wv,
              b_ig_b, b_fg_b, b_gn_g, b_skip, od_w_in, od_w_out, c_lam_q1, c_lam_k1, c_lam_q2,
              c_lam_k2, c_subln_g, final_g):
    cos, sin = rope_tables(x.shape[1], C_HEAD_DIM)
    for layer in range(DEPTH):
        hn = rmsnorm(x, norm_g[layer])
        if layer % 2 == 0:
            e = layer // 2
            x = x + even_layer(hn, ev_w_in[e], ev_w_out[e], a_ln_g[e], a_ws[e], a_bs[e], b_conv_w[e],
                               b_conv_b[e], b_wq[e], b_wk[e], b_wv[e], b_ig_b[e], b_fg_b[e], b_gn_g[e],
                               b_skip[e])
        else:
            o = layer // 2
            x = x + odd_layer(hn, od_w_in[o], od_w_out[o], c_lam_q1[o], c_lam_k1[o], c_lam_q2[o],
                              c_lam_k2[o], c_subln_g[o], lambda_init(layer), cos, sin)
    return rmsnorm(x, final_g)
```

```python
import functools
import math

import jax
import jax.numpy as jnp
from jax import lax
from jax.experimental import pallas as pl
from jax.experimental.pallas import tpu as pltpu

F32 = jnp.float32
BF16 = jnp.bfloat16

D_MODEL = 1024
A_GROUPS = 8
CHUNK = 128
B_HEADS = 4
B_HEAD_DIM = 256
B_CONV = 4
C_HEADS = 16
C_HEAD_DIM = 64
LANES = 128
ROPE_THETA = 10000.0
NORM_EPS = 1e-6
EVEN_MAIN = 6 * D_MODEL
ODD_IN = 8 * D_MODEL
GATE_LO = 5 * D_MODEL

VMEM_LIMIT_BYTES = 56 * 1024 * 1024


def _params(semantics):
    return pltpu.CompilerParams(dimension_semantics=semantics, vmem_limit_bytes=VMEM_LIMIT_BYTES)


def _rmsnorm_rows(x, g):
    return x * lax.rsqrt(jnp.mean(x * x, axis=-1, keepdims=True) + NORM_EPS) * g


def _layernorm_rows(x, g):
    xc = x - jnp.mean(x, axis=-1, keepdims=True)
    return xc * lax.rsqrt(jnp.mean(xc * xc, axis=-1, keepdims=True) + NORM_EPS) * g


def _sigmoid(x):
    return 1.0 / (1.0 + jnp.exp(-x))


def _silu(x):
    return x * _sigmoid(x)


def _dot(a, b):
    return jnp.dot(a, b, preferred_element_type=F32)


def _dot_nt(a, b):
    return lax.dot_general(a, b, (((1,), (1,)), ((), ())), preferred_element_type=F32)


def _dot_tn(a, b):
    return lax.dot_general(a, b, (((0,), (0,)), ((), ())), preferred_element_type=F32)


PROJ_TM = 512
PROJ_TN = 512


def _even_proj_kernel(x_ref, g_ref, w_ref, wg_ref, o_ref, gate_ref):
    hn = _rmsnorm_rows(x_ref[...], g_ref[...]).astype(BF16)
    gate_ref[...] = _dot(hn, wg_ref[...])
    for c in range(EVEN_MAIN // PROJ_TN):
        cols = slice(c * PROJ_TN, (c + 1) * PROJ_TN)
        o_ref[:, cols] = _dot(hn, w_ref[:, cols]).astype(BF16)


def _odd_proj_kernel(x_ref, g_ref, w_ref, cos_ref, sin_lo_ref, sin_hi_ref, o_ref):
    hn = _rmsnorm_rows(x_ref[...], g_ref[...]).astype(BF16)
    cos, sin_lo, sin_hi = cos_ref[...], sin_lo_ref[...], sin_hi_ref[...]
    q_cols = C_HEADS * 2 * C_HEAD_DIM
    for c in range(ODD_IN // PROJ_TN):
        cols = slice(c * PROJ_TN, (c + 1) * PROJ_TN)
        acc = _dot(hn, w_ref[:, cols])
        if c * PROJ_TN < 2 * q_cols:
            scale = C_HEAD_DIM ** -0.5 if c * PROJ_TN < q_cols else 1.0
            for j in range(PROJ_TN // LANES):
                blk = acc[:, j * LANES:(j + 1) * LANES]
                rot = (blk * cos + pltpu.roll(blk, LANES - C_HEAD_DIM // 2, 1) * sin_lo
                       + pltpu.roll(blk, C_HEAD_DIM // 2, 1) * sin_hi)
                o_ref[:, c * PROJ_TN + j * LANES:c * PROJ_TN + (j + 1) * LANES] = (rot * scale).astype(BF16)
        else:
            o_ref[:, cols] = acc.astype(BF16)


def _even_proj(x2, g, w_main, w_gate):
    t = x2.shape[0]
    return pl.pallas_call(
        _even_proj_kernel,
        grid=(t // PROJ_TM,),
        in_specs=[
            pl.BlockSpec((PROJ_TM, D_MODEL), lambda i: (i, 0)),
            pl.BlockSpec((1, D_MODEL), lambda i: (0, 0)),
            pl.BlockSpec((D_MODEL, EVEN_MAIN), lambda i: (0, 0), pipeline_mode=pl.Buffered(1)),
            pl.BlockSpec((D_MODEL, LANES), lambda i: (0, 0)),
        ],
        out_specs=[
            pl.BlockSpec((PROJ_TM, EVEN_MAIN), lambda i: (i, 0)),
            pl.BlockSpec((PROJ_TM, LANES), lambda i: (i, 0)),
        ],
        out_shape=[
            jax.ShapeDtypeStruct((t, EVEN_MAIN), BF16),
            jax.ShapeDtypeStruct((t, LANES), F32),
        ],
        compiler_params=_params(("parallel",)),
        name="even_proj",
    )(x2, g, w_main, w_gate)


def _odd_proj(x2, g, w, cos, sin_lo, sin_hi, seq):
    t = x2.shape[0]
    pos_blocks = seq // PROJ_TM
    tab = pl.BlockSpec((PROJ_TM, LANES), lambda i: (i % pos_blocks, 0))
    return pl.pallas_call(
        _odd_proj_kernel,
        grid=(t // PROJ_TM,),
        in_specs=[
            pl.BlockSpec((PROJ_TM, D_MODEL), lambda i: (i, 0)),
            pl.BlockSpec((1, D_MODEL), lambda i: (0, 0)),
            pl.BlockSpec((D_MODEL, ODD_IN), lambda i: (0, 0), pipeline_mode=pl.Buffered(1)),
            tab, tab, tab,
        ],
        out_specs=pl.BlockSpec((PROJ_TM, ODD_IN), lambda i: (i, 0)),
        out_shape=jax.ShapeDtypeStruct((t, ODD_IN), BF16),
        compiler_params=_params(("parallel",)),
        name="odd_proj",
    )(x2, g, w, cos, sin_lo, sin_hi)


OUT_TM = 512


def _out_proj_kernel(y_ref, w_ref, x_ref, fg_ref, o_ref, *, final):
    acc = _dot(y_ref[...], w_ref[...]) + x_ref[...]
    if final:
        acc = _rmsnorm_rows(acc, fg_ref[...])
    o_ref[...] = acc


def _out_proj(y, w, x2, final_g, final):
    t, k = y.shape
    return pl.pallas_call(
        functools.partial(_out_proj_kernel, final=final),
        grid=(t // OUT_TM,),
        in_specs=[
            pl.BlockSpec((OUT_TM, k), lambda i: (i, 0)),
            pl.BlockSpec((k, D_MODEL), lambda i: (0, 0)),
            pl.BlockSpec((OUT_TM, D_MODEL), lambda i: (i, 0)),
            pl.BlockSpec((1, D_MODEL), lambda i: (0, 0)),
        ],
        out_specs=pl.BlockSpec((OUT_TM, D_MODEL), lambda i: (i, 0)),
        out_shape=jax.ShapeDtypeStruct((t, D_MODEL), F32),
        compiler_params=_params(("parallel",)),
        name="out_proj_final" if final else "out_proj",
    )(y, w, x2, final_g)


def _even_mixer_kernel(u_ref, va_ref, za_ref, xm_ref, og_ref, zb_ref, gate_ref,
                       ln_g_ref, ws_ref, bs_ref, cw_ref, cb_ref, wq_ref, wk_ref, wv_ref,
                       gbias_ref, gn_g_ref, skip_ref,
                       o_ref, prev_ref, c_ref, n_ref, m_ref):
    chunk = pl.program_id(1)

    @pl.when(chunk == 0)
    def _():
        prev_ref[...] = jnp.zeros_like(prev_ref)
        c_ref[...] = jnp.zeros_like(c_ref)
        n_ref[...] = jnp.zeros_like(n_ref)
        m_ref[...] = jnp.zeros_like(m_ref)

    row = lax.broadcasted_iota(jnp.int32, (CHUNK, CHUNK), 0)
    col = lax.broadcasted_iota(jnp.int32, (CHUNK, CHUNK), 1)
    causal = col <= row

    u = jax.nn.gelu(u_ref[...].astype(F32))
    va = _layernorm_rows(jax.nn.gelu(va_ref[...].astype(F32)), ln_g_ref[...]).astype(BF16)
    za = za_ref[...].astype(F32)
    for g in range(A_GROUPS):
        cols = slice(g * CHUNK, (g + 1) * CHUNK)
        w = jnp.where(causal, ws_ref[g], jnp.zeros((), BF16))
        sg = _dot(w, va[:, cols]) + bs_ref[:, g:g + 1]
        o_ref[:, cols] = (u[:, cols] * sg * _silu(za[:, cols])).astype(BF16)

    xm = xm_ref[...].astype(F32)
    prev = prev_ref[...]
    prev_ref[...] = xm
    row_w = lax.broadcasted_iota(jnp.int32, (CHUNK, D_MODEL), 0)
    conv = xm * cw_ref[B_CONV - 1:B_CONV, :] + cb_ref[...]
    for j in range(1, B_CONV):
        shifted = jnp.where(row_w < j, pltpu.roll(prev, j, 0), pltpu.roll(xm, j, 0))
        conv = conv + shifted * cw_ref[B_CONV - 1 - j:B_CONV - j, :]
    xc = _silu(conv)
    xc_b = xc.astype(BF16)
    xm_b = xm_ref[...]

    pre = gate_ref[...] + gbias_ref[...]
    logf = jnp.minimum(pre, 0.0) - jnp.log(1.0 + jnp.exp(-jnp.abs(pre)))
    tri = causal.astype(BF16)
    logf_hi = logf.astype(BF16)
    logf_lo = (logf - logf_hi.astype(F32)).astype(BF16)
    bcum = _dot(tri, logf_hi) + _dot(tri, logf_lo)
    pre_t = pre.T
    bcum_t = bcum.T

    og = og_ref[...].astype(F32)
    zb = zb_ref[...].astype(F32)
    for h in range(B_HEADS):
        cols = slice(h * B_HEAD_DIM, (h + 1) * B_HEAD_DIM)
        q = _dot(xc_b[:, cols], wq_ref[h])
        k = _dot(xc_b[:, cols], wk_ref[h]) * (B_HEAD_DIM ** -0.5)
        v_b = _dot(xm_b[:, cols], wv_ref[h]).astype(BF16)
        q_b = q.astype(BF16)

        i_col = pre[:, h:h + 1]
        i_row = pre_t[h:h + 1, :]
        b_col = bcum[:, B_HEADS + h:B_HEADS + h + 1]
        b_row = bcum_t[B_HEADS + h:B_HEADS + h + 1, :]
        b_last = b_col[CHUNK - 1:CHUNK, :]
        m_prev = m_ref[h][:, 0:1]
        c_prev = c_ref[h]
        n_prev = n_ref[h]

        a_col = b_col + m_prev
        dmat = jnp.where(causal, b_col - b_row + i_row, -jnp.inf)
        m_t = jnp.maximum(a_col, jnp.max(dmat, axis=1, keepdims=True))
        sc = _dot_nt(q_b, k.astype(BF16)) * jnp.exp(dmat - m_t)
        w_inter = jnp.exp(a_col - m_t)
        num = _dot(sc.astype(BF16), v_b) + w_inter * _dot(q_b, c_prev.astype(BF16))
        den = (jnp.sum(sc, axis=1, keepdims=True)
               + w_inter * jnp.sum(q * n_prev, axis=1, keepdims=True))
        h_t = num / jnp.maximum(jnp.abs(den), jnp.exp(-m_t))

        g_col = b_last - b_col + i_col
        g_row = b_last - b_row + i_row
        m_new = jnp.maximum(b_last + m_prev, jnp.max(g_row, axis=1, keepdims=True))
        kw = k * jnp.exp(g_col - m_new)
        decay = jnp.exp(b_last + m_prev - m_new)
        c_ref[h] = decay * c_prev + _dot_tn(kw.astype(BF16), v_b)
        n_ref[h] = decay * n_prev + jnp.sum(kw, axis=0, keepdims=True)
        m_ref[h] = jnp.broadcast_to(m_new, (1, LANES))

        h_t = _layernorm_rows(h_t * _sigmoid(og[:, cols]), gn_g_ref[:, cols])
        y_b = (h_t + skip_ref[:, cols] * xc[:, cols]) * _silu(zb[:, cols])
        o_ref[:, D_MODEL + h * B_HEAD_DIM:D_MODEL + (h + 1) * B_HEAD_DIM] = y_b.astype(BF16)


def _even_mixer(proj, gates, ln_g, ws, bs_t, cw, cb, wq, wk, wv, gbias, gn_g, skip, batch, seq):
    t = proj.shape[0]
    nc = seq // CHUNK

    def col_block(j):
        return pl.BlockSpec((CHUNK, D_MODEL), lambda b, c, j=j: (b * nc + c, j))

    def whole(a):
        nd = a.ndim
        return pl.BlockSpec(a.shape, lambda b, c, nd=nd: (0,) * nd)

    consts = (ln_g, ws, bs_t, cw, cb, wq, wk, wv, gbias, gn_g, skip)
    return pl.pallas_call(
        _even_mixer_kernel,
        grid=(batch, nc),
        in_specs=[col_block(j) for j in range(6)]
        + [pl.BlockSpec((CHUNK, LANES), lambda b, c: (b * nc + c, 0))]
        + [whole(a) for a in consts],
        out_specs=pl.BlockSpec((CHUNK, 2 * D_MODEL), lambda b, c: (b * nc + c, 0)),
        out_shape=jax.ShapeDtypeStruct((t, 2 * D_MODEL), BF16),
        scratch_shapes=[
            pltpu.VMEM((CHUNK, D_MODEL), F32),
            pltpu.VMEM((B_HEADS, B_HEAD_DIM, B_HEAD_DIM), F32),
            pltpu.VMEM((B_HEADS, 1, B_HEAD_DIM), F32),
            pltpu.VMEM((B_HEADS, 1, LANES), F32),
        ],
        compiler_params=_params(("parallel", "arbitrary")),
        name="even_mixer",
    )(proj, proj, proj, proj, proj, proj, gates, *consts)


ATTN_T = 256


def _attn_kernel(lamp_ref, subln_ref, q_ref, k_ref, v_ref, z_ref, o_ref, m_ref, l_ref, acc_ref,
                 *, tile, lam_init):
    seq = q_ref.shape[0]
    lp = lamp_ref[...]
    lam = (jnp.exp(jnp.sum(lp[0:1] * lp[1:2], axis=-1, keepdims=True))
           - jnp.exp(jnp.sum(lp[2:3] * lp[3:4], axis=-1, keepdims=True)) + lam_init)
    lane = lax.broadcasted_iota(jnp.int32, (tile, LANES), 1)
    row = lax.broadcasted_iota(jnp.int32, (tile, tile), 0)
    col = lax.broadcasted_iota(jnp.int32, (tile, tile), 1)
    zero = jnp.zeros((), BF16)

    def q_tile(qi, carry):
        q0 = pl.multiple_of(qi * tile, tile)
        q = q_ref[pl.ds(q0, tile), :]
        qs = (jnp.where(lane < C_HEAD_DIM, q, zero), jnp.where(lane >= C_HEAD_DIM, q, zero))
        m_ref[...] = jnp.full_like(m_ref, -jnp.inf)
        l_ref[...] = jnp.zeros_like(l_ref)
        acc_ref[...] = jnp.zeros_like(acc_ref)

        def kv_step(k0, masked):
            k = k_ref[pl.ds(k0, tile), :]
            v = v_ref[pl.ds(k0, tile), :]
            for c in range(2):
                s = _dot_nt(qs[c], k)
                if masked:
                    s = jnp.where(col <= row, s, -jnp.inf)
                m_prev = m_ref[c]
                m_new = jnp.maximum(m_prev, jnp.max(s, axis=1, keepdims=True))
                alpha = jnp.exp(m_prev - m_new)
                p = jnp.exp(s - jnp.concatenate([m_new] * (tile // LANES), axis=1))
                l_ref[c] = alpha * l_ref[c] + jnp.sum(p, axis=1, keepdims=True)
                acc_ref[c] = alpha * acc_ref[c] + _dot(p.astype(BF16), v)
                m_ref[c] = m_new

        def off_diag(ki, c):
            kv_step(pl.multiple_of(ki * tile, tile), False)
            return c

        lax.fori_loop(0, qi, off_diag, 0)
        kv_step(q0, True)

        o = acc_ref[0] / l_ref[0] - lam * (acc_ref[1] / l_ref[1])
        o = _rmsnorm_rows(o, subln_ref[...]) * (1.0 - lam_init)
        z = z_ref[pl.ds(q0, tile), :].astype(F32)
        o_ref[pl.ds(q0, tile), :] = (o * _silu(z)).astype(BF16)
        return carry

    lax.fori_loop(0, seq // tile, q_tile, 0)


def _attention(proj, lamp, subln, batch, seq, lam_init):
    t = proj.shape[0]
    tile = min(ATTN_T, seq)

    def head_block(part):
        return pl.BlockSpec((seq, LANES), lambda b, h, part=part: (b, part * C_HEADS + h))

    return pl.pallas_call(
        functools.partial(_attn_kernel, tile=tile, lam_init=lam_init),
        grid=(batch, C_HEADS),
        in_specs=[
            pl.BlockSpec(lamp.shape, lambda b, h: (0, 0)),
            pl.BlockSpec(subln.shape, lambda b, h: (0, 0)),
            head_block(0), head_block(1), head_block(2), head_block(3),
        ],
        out_specs=pl.BlockSpec((seq, LANES), lambda b, h: (b, h)),
        out_shape=jax.ShapeDtypeStruct((t, C_HEADS * LANES), BF16),
        scratch_shapes=[
            pltpu.VMEM((2, tile, LANES), F32),
            pltpu.VMEM((2, tile, LANES), F32),
            pltpu.VMEM((2, tile, LANES), F32),
        ],
        compiler_params=_params(("parallel", "parallel")),
        name="diff_attention",
    )(lamp, subln, proj, proj, proj, proj)


def _lambda_init(layer):
    return 0.8 - 0.6 * math.exp(-0.3 * layer)


def _rope_tables(seq):
    half = C_HEAD_DIM // 2
    inv = ROPE_THETA ** (-jnp.arange(0, C_HEAD_DIM, 2, dtype=F32) / C_HEAD_DIM)
    ang = jnp.arange(seq, dtype=F32)[:, None] * inv[None, :]
    cos = jnp.tile(jnp.cos(ang), (1, LANES // half))
    sin = jnp.tile(jnp.sin(ang), (1, LANES // half))
    first_half = (jnp.arange(LANES) % C_HEAD_DIM) < half
    return cos, jnp.where(first_half, -sin, 0.0), jnp.where(first_half, 0.0, sin)


def kernel(x, norm_g, ev_w_in, ev_w_out, a_ln_g, a_ws, a_bs, b_conv_w, b_conv_b, b_wq, b_wk, b_wv,
           b_ig_b, b_fg_b, b_gn_g, b_skip, od_w_in, od_w_out, c_lam_q1, c_lam_k1, c_lam_q2,
           c_lam_k2, c_subln_g, final_g):
    batch, seq, d = x.shape
    depth = norm_g.shape[0]
    assert d == D_MODEL and seq % PROJ_TM == 0 and (batch * seq) % OUT_TM == 0
    x2 = x.reshape(batch * seq, d)
    cos, sin_lo, sin_hi = _rope_tables(seq)
    fg = final_g.reshape(1, d)
    row = lambda a: a.reshape(1, -1)
    for layer in range(depth):
        g = row(norm_g[layer])
        final = layer == depth - 1
        if layer % 2 == 0:
            e = layer // 2
            w_in = ev_w_in[e]
            w_main = jnp.concatenate([w_in[:, :GATE_LO], w_in[:, GATE_LO + 2 * B_HEADS:]], axis=1).astype(BF16)
            w_gate = jnp.pad(w_in[:, GATE_LO:GATE_LO + 2 * B_HEADS], ((0, 0), (0, LANES - 2 * B_HEADS))).astype(BF16)
            proj, gates = _even_proj(x2, g, w_main, w_gate)
            gbias = jnp.pad(jnp.concatenate([b_ig_b[e], b_fg_b[e]]), (0, LANES - 2 * B_HEADS)).reshape(1, LANES)
            y = _even_mixer(proj, gates, row(a_ln_g[e]), a_ws[e].astype(BF16), a_bs[e].T, b_conv_w[e],
                            row(b_conv_b[e]), b_wq[e].astype(BF16), b_wk[e].astype(BF16),
                            b_wv[e].astype(BF16), gbias, row(b_gn_g[e]), row(b_skip[e]), batch, seq)
            x2 = _out_proj(y, ev_w_out[e].astype(BF16), x2, fg, final)
        else:
            o = layer // 2
            proj = _odd_proj(x2, g, od_w_in[o].astype(BF16), cos, sin_lo, sin_hi, seq)
            lamp = jnp.stack([c_lam_q1[o], c_lam_k1[o], c_lam_q2[o], c_lam_k2[o]])
            y = _attention(proj, lamp, row(c_subln_g[o]), batch, seq, _lambda_init(layer))
            x2 = _out_proj(y, od_w_out[o].astype(BF16), x2, fg, final)
    return x2.reshape(batch, seq, d)
```

```python
import functools
import math

import jax
import jax.numpy as jnp
from jax import lax
from jax.experimental import pallas as pl
from jax.experimental.pallas import tpu as pltpu

F32 = jnp.float32
BF16 = jnp.bfloat16

D_MODEL = 1024
A_GROUPS = 8
CHUNK = 128
B_HEADS = 4
B_HEAD_DIM = 256
B_CONV = 4
C_HEADS = 16
C_HEAD_DIM = 64
LANES = 128
ROPE_THETA = 10000.0
NORM_EPS = 1e-6
LOG2E = 1.4426950408889634
EVEN_MAIN = 6 * D_MODEL
ODD_IN = 8 * D_MODEL
GATE_LO = 5 * D_MODEL

VMEM_LIMIT_BYTES = 56 * 1024 * 1024


def _params(semantics):
    return pltpu.CompilerParams(dimension_semantics=semantics, vmem_limit_bytes=VMEM_LIMIT_BYTES)


def _rmsnorm_rows(x, g):
    return x * lax.rsqrt(jnp.mean(x * x, axis=-1, keepdims=True) + NORM_EPS) * g


def _layernorm_rows(x, g):
    xc = x - jnp.mean(x, axis=-1, keepdims=True)
    return xc * lax.rsqrt(jnp.mean(xc * xc, axis=-1, keepdims=True) + NORM_EPS) * g


def _sigmoid(x):
    return 1.0 / (1.0 + jnp.exp(-x))


def _silu(x):
    return x * _sigmoid(x)


def _dot(a, b):
    return jnp.dot(a, b, preferred_element_type=F32)


def _dot_nt(a, b):
    return lax.dot_general(a, b, (((1,), (1,)), ((), ())), preferred_element_type=F32)


def _dot_tn(a, b):
    return lax.dot_general(a, b, (((0,), (0,)), ((), ())), preferred_element_type=F32)


PROJ_TM = 512
PROJ_TN = 512


def _even_proj_kernel(x_ref, g_ref, w_ref, wg_ref, o_ref, gate_ref):
    hn = _rmsnorm_rows(x_ref[...], g_ref[...]).astype(BF16)
    gate_ref[...] = _dot(hn, wg_ref[...])
    for c in range(EVEN_MAIN // PROJ_TN):
        cols = slice(c * PROJ_TN, (c + 1) * PROJ_TN)
        o_ref[:, cols] = _dot(hn, w_ref[:, cols]).astype(BF16)


def _odd_proj_kernel(x_ref, g_ref, w_ref, cos_ref, sin_lo_ref, sin_hi_ref, o_ref):
    hn = _rmsnorm_rows(x_ref[...], g_ref[...]).astype(BF16)
    cos, sin_lo, sin_hi = cos_ref[...], sin_lo_ref[...], sin_hi_ref[...]
    q_cols = C_HEADS * 2 * C_HEAD_DIM
    for c in range(ODD_IN // PROJ_TN):
        cols = slice(c * PROJ_TN, (c + 1) * PROJ_TN)
        acc = _dot(hn, w_ref[:, cols])
        if c * PROJ_TN < 2 * q_cols:
            scale = C_HEAD_DIM ** -0.5 * LOG2E if c * PROJ_TN < q_cols else 1.0
            for j in range(PROJ_TN // LANES):
                blk = acc[:, j * LANES:(j + 1) * LANES]
                rot = (blk * cos + pltpu.roll(blk, LANES - C_HEAD_DIM // 2, 1) * sin_lo
                       + pltpu.roll(blk, C_HEAD_DIM // 2, 1) * sin_hi)
                o_ref[:, c * PROJ_TN + j * LANES:c * PROJ_TN + (j + 1) * LANES] = (rot * scale).astype(BF16)
        else:
            o_ref[:, cols] = acc.astype(BF16)


def _even_proj(x2, g, w_main, w_gate):
    t = x2.shape[0]
    return pl.pallas_call(
        _even_proj_kernel,
        grid=(t // PROJ_TM,),
        in_specs=[
            pl.BlockSpec((PROJ_TM, D_MODEL), lambda i: (i, 0)),
            pl.BlockSpec((1, D_MODEL), lambda i: (0, 0)),
            pl.BlockSpec((D_MODEL, EVEN_MAIN), lambda i: (0, 0), pipeline_mode=pl.Buffered(1)),
            pl.BlockSpec((D_MODEL, LANES), lambda i: (0, 0)),
        ],
        out_specs=[
            pl.BlockSpec((PROJ_TM, EVEN_MAIN), lambda i: (i, 0)),
            pl.BlockSpec((PROJ_TM, LANES), lambda i: (i, 0)),
        ],
        out_shape=[
            jax.ShapeDtypeStruct((t, EVEN_MAIN), BF16),
            jax.ShapeDtypeStruct((t, LANES), F32),
        ],
        compiler_params=_params(("parallel",)),
        name="even_proj",
    )(x2, g, w_main, w_gate)


def _odd_proj(x2, g, w, cos, sin_lo, sin_hi, seq):
    t = x2.shape[0]
    pos_blocks = seq // PROJ_TM
    tab = pl.BlockSpec((PROJ_TM, LANES), lambda i: (i % pos_blocks, 0))
    return pl.pallas_call(
        _odd_proj_kernel,
        grid=(t // PROJ_TM,),
        in_specs=[
            pl.BlockSpec((PROJ_TM, D_MODEL), lambda i: (i, 0)),
            pl.BlockSpec((1, D_MODEL), lambda i: (0, 0)),
            pl.BlockSpec((D_MODEL, ODD_IN), lambda i: (0, 0), pipeline_mode=pl.Buffered(1)),
            tab, tab, tab,
        ],
        out_specs=pl.BlockSpec((PROJ_TM, ODD_IN), lambda i: (i, 0)),
        out_shape=jax.ShapeDtypeStruct((t, ODD_IN), BF16),
        compiler_params=_params(("parallel",)),
        name="odd_proj",
    )(x2, g, w, cos, sin_lo, sin_hi)


OUT_TM = 512


def _out_proj_kernel(y_ref, w_ref, x_ref, fg_ref, o_ref, *, final):
    acc = _dot(y_ref[...], w_ref[...]) + x_ref[...]
    if final:
        acc = _rmsnorm_rows(acc, fg_ref[...])
    o_ref[...] = acc


def _out_proj(y, w, x2, final_g, final):
    t, k = y.shape
    return pl.pallas_call(
        functools.partial(_out_proj_kernel, final=final),
        grid=(t // OUT_TM,),
        in_specs=[
            pl.BlockSpec((OUT_TM, k), lambda i: (i, 0)),
            pl.BlockSpec((k, D_MODEL), lambda i: (0, 0)),
            pl.BlockSpec((OUT_TM, D_MODEL), lambda i: (i, 0)),
            pl.BlockSpec((1, D_MODEL), lambda i: (0, 0)),
        ],
        out_specs=pl.BlockSpec((OUT_TM, D_MODEL), lambda i: (i, 0)),
        out_shape=jax.ShapeDtypeStruct((t, D_MODEL), F32),
        compiler_params=_params(("parallel",)),
        name="out_proj_final" if final else "out_proj",
    )(y, w, x2, final_g)


def _even_mixer_kernel(u_ref, va_ref, za_ref, xm_ref, og_ref, zb_ref, gate_ref,
                       ln_g_ref, ws_ref, bs_ref, cw_ref, cb_ref, wq_ref, wk_ref, wv_ref,
                       gbias_ref, gn_g_ref, skip_ref,
                       o_ref, prev_ref, c_ref, n_ref, m_ref):
    chunk = pl.program_id(1)

    @pl.when(chunk == 0)
    def _():
        prev_ref[...] = jnp.zeros_like(prev_ref)
        c_ref[...] = jnp.zeros_like(c_ref)
        n_ref[...] = jnp.zeros_like(n_ref)
        m_ref[...] = jnp.zeros_like(m_ref)

    row = lax.broadcasted_iota(jnp.int32, (CHUNK, CHUNK), 0)
    col = lax.broadcasted_iota(jnp.int32, (CHUNK, CHUNK), 1)
    causal = col <= row

    u = jax.nn.gelu(u_ref[...].astype(F32))
    va = _layernorm_rows(jax.nn.gelu(va_ref[...].astype(F32)), ln_g_ref[...]).astype(BF16)
    za = za_ref[...].astype(F32)
    for g in range(A_GROUPS):
        cols = slice(g * CHUNK, (g + 1) * CHUNK)
        w = jnp.where(causal, ws_ref[g], jnp.zeros((), BF16))
        sg = _dot(w, va[:, cols]) + bs_ref[:, g:g + 1]
        o_ref[:, cols] = (u[:, cols] * sg * _silu(za[:, cols])).astype(BF16)

    xm = xm_ref[...].astype(F32)
    prev = prev_ref[...]
    prev_ref[...] = xm
    row_w = lax.broadcasted_iota(jnp.int32, (CHUNK, D_MODEL), 0)
    conv = xm * cw_ref[B_CONV - 1:B_CONV, :] + cb_ref[...]
    for j in range(1, B_CONV):
        shifted = jnp.where(row_w < j, pltpu.roll(prev, j, 0), pltpu.roll(xm, j, 0))
        conv = conv + shifted * cw_ref[B_CONV - 1 - j:B_CONV - j, :]
    xc = _silu(conv)
    xc_b = xc.astype(BF16)
    xm_b = xm_ref[...]

    pre = gate_ref[...] + gbias_ref[...]
    logf = jnp.minimum(pre, 0.0) - jnp.log(1.0 + jnp.exp(-jnp.abs(pre)))
    tri = causal.astype(BF16)
    logf_hi = logf.astype(BF16)
    logf_lo = (logf - logf_hi.astype(F32)).astype(BF16)
    bcum = _dot(tri, logf_hi) + _dot(tri, logf_lo)
    pre_t = pre.T
    bcum_t = bcum.T

    og = og_ref[...].astype(F32)
    zb = zb_ref[...].astype(F32)
    for h in range(B_HEADS):
        cols = slice(h * B_HEAD_DIM, (h + 1) * B_HEAD_DIM)
        q = _dot(xc_b[:, cols], wq_ref[h])
        k = _dot(xc_b[:, cols], wk_ref[h]) * (B_HEAD_DIM ** -0.5)
        v_b = _dot(xm_b[:, cols], wv_ref[h]).astype(BF16)
        q_b = q.astype(BF16)

        i_col = pre[:, h:h + 1]
        i_row = pre_t[h:h + 1, :]
        b_col = bcum[:, B_HEADS + h:B_HEADS + h + 1]
        b_row = bcum_t[B_HEADS + h:B_HEADS + h + 1, :]
        b_last = b_col[CHUNK - 1:CHUNK, :]
        m_prev = m_ref[h][:, 0:1]
        c_prev = c_ref[h]
        n_prev = n_ref[h]

        a_col = b_col + m_prev
        dmat = jnp.where(causal, b_col - b_row + i_row, -jnp.inf)
        m_t = jnp.maximum(a_col, jnp.max(dmat, axis=1, keepdims=True))
        sc = _dot_nt(q_b, k.astype(BF16)) * jnp.exp(dmat - m_t)
        w_inter = jnp.exp(a_col - m_t)
        num = _dot(sc.astype(BF16), v_b) + w_inter * _dot(q_b, c_prev.astype(BF16))
        den = (jnp.sum(sc, axis=1, keepdims=True)
               + w_inter * jnp.sum(q * n_prev, axis=1, keepdims=True))
        h_t = num / jnp.maximum(jnp.abs(den), jnp.exp(-m_t))

        g_col = b_last - b_col + i_col
        g_row = b_last - b_row + i_row
        m_new = jnp.maximum(b_last + m_prev, jnp.max(g_row, axis=1, keepdims=True))
        kw = k * jnp.exp(g_col - m_new)
        decay = jnp.exp(b_last + m_prev - m_new)
        c_ref[h] = decay * c_prev + _dot_tn(kw.astype(BF16), v_b)
        n_ref[h] = decay * n_prev + jnp.sum(kw, axis=0, keepdims=True)
        m_ref[h] = jnp.broadcast_to(m_new, (1, LANES))

        h_t = _layernorm_rows(h_t * _sigmoid(og[:, cols]), gn_g_ref[:, cols])
        y_b = (h_t + skip_ref[:, cols] * xc[:, cols]) * _silu(zb[:, cols])
        o_ref[:, D_MODEL + h * B_HEAD_DIM:D_MODEL + (h + 1) * B_HEAD_DIM] = y_b.astype(BF16)


def _even_mixer(proj, gates, ln_g, ws, bs_t, cw, cb, wq, wk, wv, gbias, gn_g, skip, batch, seq):
    t = proj.shape[0]
    nc = seq // CHUNK

    def col_block(j):
        return pl.BlockSpec((CHUNK, D_MODEL), lambda b, c, j=j: (b * nc + c, j))

    def whole(a):
        nd = a.ndim
        return pl.BlockSpec(a.shape, lambda b, c, nd=nd: (0,) * nd)

    consts = (ln_g, ws, bs_t, cw, cb, wq, wk, wv, gbias, gn_g, skip)
    return pl.pallas_call(
        _even_mixer_kernel,
        grid=(batch, nc),
        in_specs=[col_block(j) for j in range(6)]
        + [pl.BlockSpec((CHUNK, LANES), lambda b, c: (b * nc + c, 0))]
        + [whole(a) for a in consts],
        out_specs=pl.BlockSpec((CHUNK, 2 * D_MODEL), lambda b, c: (b * nc + c, 0)),
        out_shape=jax.ShapeDtypeStruct((t, 2 * D_MODEL), BF16),
        scratch_shapes=[
            pltpu.VMEM((CHUNK, D_MODEL), F32),
            pltpu.VMEM((B_HEADS, B_HEAD_DIM, B_HEAD_DIM), F32),
            pltpu.VMEM((B_HEADS, 1, B_HEAD_DIM), F32),
            pltpu.VMEM((B_HEADS, 1, LANES), F32),
        ],
        compiler_params=_params(("parallel", "arbitrary")),
        name="even_mixer",
    )(proj, proj, proj, proj, proj, proj, gates, *consts)


ATTN_TQ = 512
ATTN_TK = 512


def _attn_kernel(lamp_ref, subln_ref, q_ref, k_ref, v_ref, z_ref, o_ref, vext_ref, m_ref, acc_ref,
                 *, tq, tk, lam_init):
    seq = q_ref.shape[0]
    lp = lamp_ref[...]
    lam = (jnp.exp(jnp.sum(lp[0:1] * lp[1:2], axis=-1, keepdims=True))
           - jnp.exp(jnp.sum(lp[2:3] * lp[3:4], axis=-1, keepdims=True)) + lam_init)
    lane = lax.broadcasted_iota(jnp.int32, (tq, LANES), 1)
    row = lax.broadcasted_iota(jnp.int32, (tq, tk), 0)
    col = lax.broadcasted_iota(jnp.int32, (tq, tk), 1)
    zero = jnp.zeros((), BF16)

    vext_ref[:, :LANES] = v_ref[...]
    vext_ref[:, LANES:] = jnp.ones((seq, LANES), BF16)

    def q_tile(qi, carry):
        q0 = pl.multiple_of(qi * tq, tq)
        q = q_ref[pl.ds(q0, tq), :]
        qs = (jnp.where(lane < C_HEAD_DIM, q, zero), jnp.where(lane >= C_HEAD_DIM, q, zero))
        m_ref[...] = jnp.full_like(m_ref, -jnp.inf)
        acc_ref[...] = jnp.zeros_like(acc_ref)

        def kv_step(k0, diag_offset):
            k = k_ref[pl.ds(k0, tk), :]
            vext = vext_ref[pl.ds(k0, tk), :]
            for c in range(2):
                s = _dot_nt(qs[c], k)
                if diag_offset is not None:
                    s = jnp.where(col + diag_offset <= row, s, -jnp.inf)
                m_prev = m_ref[c]
                m_new = jnp.maximum(m_prev, jnp.max(s, axis=1, keepdims=True))
                alpha = jnp.exp2(m_prev - m_new)
                p = jnp.exp2(s - jnp.concatenate([m_new] * (tk // LANES), axis=1))
                acc_ref[c] = (jnp.concatenate([alpha, alpha], axis=1) * acc_ref[c]
                              + _dot(p.astype(BF16), vext))
                m_ref[c] = m_new

        def off_diag(ki, c):
            kv_step(pl.multiple_of(ki * tk, tk), None)
            return c

        lax.fori_loop(0, qi * (tq // tk), off_diag, 0)
        for d in range(tq // tk):
            kv_step(pl.multiple_of(q0 + d * tk, tk), d * tk)

        a0, a1 = acc_ref[0], acc_ref[1]
        o = a0[:, :LANES] / a0[:, LANES:] - lam * (a1[:, :LANES] / a1[:, LANES:])
        o = _rmsnorm_rows(o, subln_ref[...]) * (1.0 - lam_init)
        z = z_ref[pl.ds(q0, tq), :].astype(F32)
        o_ref[pl.ds(q0, tq), :] = (o * _silu(z)).astype(BF16)
        return carry

    lax.fori_loop(0, seq // tq, q_tile, 0)


def _attention(proj, lamp, subln, batch, seq, lam_init):
    t = proj.shape[0]
    tq, tk = min(ATTN_TQ, seq), min(ATTN_TK, seq)

    def head_block(part):
        return pl.BlockSpec((seq, LANES), lambda b, h, part=part: (b, part * C_HEADS + h))

    return pl.pallas_call(
        functools.partial(_attn_kernel, tq=tq, tk=tk, lam_init=lam_init),
        grid=(batch, C_HEADS),
        in_specs=[
            pl.BlockSpec(lamp.shape, lambda b, h: (0, 0)),
            pl.BlockSpec(subln.shape, lambda b, h: (0, 0)),
            head_block(0), head_block(1), head_block(2), head_block(3),
        ],
        out_specs=pl.BlockSpec((seq, LANES), lambda b, h: (b, h)),
        out_shape=jax.ShapeDtypeStruct((t, C_HEADS * LANES), BF16),
        scratch_shapes=[
            pltpu.VMEM((seq, 2 * LANES), BF16),
            pltpu.VMEM((2, tq, LANES), F32),
            pltpu.VMEM((2, tq, 2 * LANES), F32),
        ],
        compiler_params=_params(("parallel", "parallel")),
        name="diff_attention",
    )(lamp, subln, proj, proj, proj, proj)


def _lambda_init(layer):
    return 0.8 - 0.6 * math.exp(-0.3 * layer)


def _rope_tables(seq):
    half = C_HEAD_DIM // 2
    inv = ROPE_THETA ** (-jnp.arange(0, C_HEAD_DIM, 2, dtype=F32) / C_HEAD_DIM)
    ang = jnp.arange(seq, dtype=F32)[:, None] * inv[None, :]
    cos = jnp.tile(jnp.cos(ang), (1, LANES // half))
    sin = jnp.tile(jnp.sin(ang), (1, LANES // half))
    first_half = (jnp.arange(LANES) % C_HEAD_DIM) < half
    return cos, jnp.where(first_half, -sin, 0.0), jnp.where(first_half, 0.0, sin)


def kernel(x, norm_g, ev_w_in, ev_w_out, a_ln_g, a_ws, a_bs, b_conv_w, b_conv_b, b_wq, b_wk, b_wv,
           b_ig_b, b_fg_b, b_gn_g, b_skip, od_w_in, od_w_out, c_lam_q1, c_lam_k1, c_lam_q2,
           c_lam_k2, c_subln_g, final_g):
    batch, seq, d = x.shape
    depth = norm_g.shape[0]
    assert d == D_MODEL and seq % PROJ_TM == 0 and (batch * seq) % OUT_TM == 0
    x2 = x.reshape(batch * seq, d)
    cos, sin_lo, sin_hi = _rope_tables(seq)
    fg = final_g.reshape(1, d)
    row = lambda a: a.reshape(1, -1)
    for layer in range(depth):
        g = row(norm_g[layer])
        final = layer == depth - 1
        if layer % 2 == 0:
            e = layer // 2
            w_in = ev_w_in[e]
            w_main = jnp.concatenate([w_in[:, :GATE_LO], w_in[:, GATE_LO + 2 * B_HEADS:]], axis=1).astype(BF16)
            w_gate = jnp.pad(w_in[:, GATE_LO:GATE_LO + 2 * B_HEADS], ((0, 0), (0, LANES - 2 * B_HEADS))).astype(BF16)
            proj, gates = _even_proj(x2, g, w_main, w_gate)
            gbias = jnp.pad(jnp.concatenate([b_ig_b[e], b_fg_b[e]]), (0, LANES - 2 * B_HEADS)).reshape(1, LANES)
            y = _even_mixer(proj, gates, row(a_ln_g[e]), a_ws[e].astype(BF16), a_bs[e].T, b_conv_w[e],
                            row(b_conv_b[e]), b_wq[e].astype(BF16), b_wk[e].astype(BF16),
                            b_wv[e].astype(BF16), gbias, row(b_gn_g[e]), row(b_skip[e]), batch, seq)
            x2 = _out_proj(y, ev_w_out[e].astype(BF16), x2, fg, final)
        else:
            o = layer // 2
            proj = _odd_proj(x2, g, od_w_in[o].astype(BF16), cos, sin_lo, sin_hi, seq)
            lamp = jnp.stack([c_lam_q1[o], c_lam_k1[o], c_lam_q2[o], c_lam_k2[o]])
            y = _attention(proj, lamp, row(c_subln_g[o]), batch, seq, _lambda_init(layer))
            x2 = _out_proj(y, od_w_out[o].astype(BF16), x2, fg, final)
    return x2.reshape(batch, seq, d)
```

```python
import functools
import math

import jax
import jax.numpy as jnp
from jax import lax
from jax.experimental import pallas as pl
from jax.experimental.pallas import tpu as pltpu

F32 = jnp.float32
BF16 = jnp.bfloat16

D_MODEL = 1024
A_GROUPS = 8
CHUNK = 128
B_HEADS = 4
B_HEAD_DIM = 256
B_CONV = 4
C_HEADS = 16
C_HEAD_DIM = 64
LANES = 128
ROPE_THETA = 10000.0
NORM_EPS = 1e-6
LOG2E = 1.4426950408889634
EVEN_MAIN = 6 * D_MODEL
ODD_IN = 8 * D_MODEL
GATE_LO = 5 * D_MODEL

VMEM_LIMIT_BYTES = 56 * 1024 * 1024


def _params(semantics):
    return pltpu.CompilerParams(dimension_semantics=semantics, vmem_limit_bytes=VMEM_LIMIT_BYTES)


def _rmsnorm_rows(x, g):
    return x * lax.rsqrt(jnp.mean(x * x, axis=-1, keepdims=True) + NORM_EPS) * g


def _layernorm_rows(x, g):
    xc = x - jnp.mean(x, axis=-1, keepdims=True)
    return xc * lax.rsqrt(jnp.mean(xc * xc, axis=-1, keepdims=True) + NORM_EPS) * g


def _sigmoid(x):
    return 1.0 / (1.0 + jnp.exp(-x))


def _silu(x):
    return x * _sigmoid(x)


def _dot(a, b):
    return jnp.dot(a, b, preferred_element_type=F32)


def _dot_nt(a, b):
    return lax.dot_general(a, b, (((1,), (1,)), ((), ())), preferred_element_type=F32)


def _dot_tn(a, b):
    return lax.dot_general(a, b, (((0,), (0,)), ((), ())), preferred_element_type=F32)


PROJ_TM = 512
PROJ_TN = 512


def _even_proj_kernel(x_ref, g_ref, w_ref, wg_ref, o_ref, gate_ref):
    hn = _rmsnorm_rows(x_ref[...], g_ref[...]).astype(BF16)
    gate_ref[...] = _dot(hn, wg_ref[...])
    for c in range(EVEN_MAIN // PROJ_TN):
        cols = slice(c * PROJ_TN, (c + 1) * PROJ_TN)
        o_ref[:, cols] = _dot(hn, w_ref[:, cols]).astype(BF16)


def _odd_proj_kernel(x_ref, g_ref, w_ref, cos_ref, sin_lo_ref, sin_hi_ref, o_ref):
    hn = _rmsnorm_rows(x_ref[...], g_ref[...]).astype(BF16)
    cos, sin_lo, sin_hi = cos_ref[...], sin_lo_ref[...], sin_hi_ref[...]
    q_cols = C_HEADS * 2 * C_HEAD_DIM
    for c in range(ODD_IN // PROJ_TN):
        cols = slice(c * PROJ_TN, (c + 1) * PROJ_TN)
        acc = _dot(hn, w_ref[:, cols])
        if c * PROJ_TN < 2 * q_cols:
            scale = C_HEAD_DIM ** -0.5 * LOG2E if c * PROJ_TN < q_cols else 1.0
            for j in range(PROJ_TN // LANES):
                blk = acc[:, j * LANES:(j + 1) * LANES]
                rot = (blk * cos + pltpu.roll(blk, LANES - C_HEAD_DIM // 2, 1) * sin_lo
                       + pltpu.roll(blk, C_HEAD_DIM // 2, 1) * sin_hi)
                o_ref[:, c * PROJ_TN + j * LANES:c * PROJ_TN + (j + 1) * LANES] = (rot * scale).astype(BF16)
        else:
            o_ref[:, cols] = acc.astype(BF16)


def _even_proj(x2, g, w_main, w_gate):
    t = x2.shape[0]
    return pl.pallas_call(
        _even_proj_kernel,
        grid=(t // PROJ_TM,),
        in_specs=[
            pl.BlockSpec((PROJ_TM, D_MODEL), lambda i: (i, 0)),
            pl.BlockSpec((1, D_MODEL), lambda i: (0, 0)),
            pl.BlockSpec((D_MODEL, EVEN_MAIN), lambda i: (0, 0), pipeline_mode=pl.Buffered(1)),
            pl.BlockSpec((D_MODEL, LANES), lambda i: (0, 0)),
        ],
        out_specs=[
            pl.BlockSpec((PROJ_TM, EVEN_MAIN), lambda i: (i, 0)),
            pl.BlockSpec((PROJ_TM, LANES), lambda i: (i, 0)),
        ],
        out_shape=[
            jax.ShapeDtypeStruct((t, EVEN_MAIN), BF16),
            jax.ShapeDtypeStruct((t, LANES), F32),
        ],
        compiler_params=_params(("parallel",)),
        name="even_proj",
    )(x2, g, w_main, w_gate)


def _odd_proj(x2, g, w, cos, sin_lo, sin_hi, seq):
    t = x2.shape[0]
    pos_blocks = seq // PROJ_TM
    tab = pl.BlockSpec((PROJ_TM, LANES), lambda i: (i % pos_blocks, 0))
    return pl.pallas_call(
        _odd_proj_kernel,
        grid=(t // PROJ_TM,),
        in_specs=[
            pl.BlockSpec((PROJ_TM, D_MODEL), lambda i: (i, 0)),
            pl.BlockSpec((1, D_MODEL), lambda i: (0, 0)),
            pl.BlockSpec((D_MODEL, ODD_IN), lambda i: (0, 0), pipeline_mode=pl.Buffered(1)),
            tab, tab, tab,
        ],
        out_specs=pl.BlockSpec((PROJ_TM, ODD_IN), lambda i: (i, 0)),
        out_shape=jax.ShapeDtypeStruct((t, ODD_IN), BF16),
        compiler_params=_params(("parallel",)),
        name="odd_proj",
    )(x2, g, w, cos, sin_lo, sin_hi)


OUT_TM = 512


def _out_proj_kernel(y_ref, w_ref, x_ref, fg_ref, o_ref, *, final):
    acc = _dot(y_ref[...], w_ref[...]) + x_ref[...]
    if final:
        acc = _rmsnorm_rows(acc, fg_ref[...])
    o_ref[...] = acc


def _out_proj(y, w, x2, final_g, final):
    t, k = y.shape
    return pl.pallas_call(
        functools.partial(_out_proj_kernel, final=final),
        grid=(t // OUT_TM,),
        in_specs=[
            pl.BlockSpec((OUT_TM, k), lambda i: (i, 0)),
            pl.BlockSpec((k, D_MODEL), lambda i: (0, 0)),
            pl.BlockSpec((OUT_TM, D_MODEL), lambda i: (i, 0)),
            pl.BlockSpec((1, D_MODEL), lambda i: (0, 0)),
        ],
        out_specs=pl.BlockSpec((OUT_TM, D_MODEL), lambda i: (i, 0)),
        out_shape=jax.ShapeDtypeStruct((t, D_MODEL), F32),
        compiler_params=_params(("parallel",)),
        name="out_proj_final" if final else "out_proj",
    )(y, w, x2, final_g)


def _even_mixer_kernel(u_ref, va_ref, za_ref, xm_ref, og_ref, zb_ref, gate_ref,
                       ln_g_ref, ws_ref, bs_ref, cw_ref, cb_ref, wq_ref, wk_ref, wv_ref,
                       gbias_ref, gn_g_ref, skip_ref,
                       o_ref, prev_ref, c_ref, n_ref, m_ref):
    chunk = pl.program_id(1)

    @pl.when(chunk == 0)
    def _():
        prev_ref[...] = jnp.zeros_like(prev_ref)
        c_ref[...] = jnp.zeros_like(c_ref)
        n_ref[...] = jnp.zeros_like(n_ref)
        m_ref[...] = jnp.zeros_like(m_ref)

    row = lax.broadcasted_iota(jnp.int32, (CHUNK, CHUNK), 0)
    col = lax.broadcasted_iota(jnp.int32, (CHUNK, CHUNK), 1)
    causal = col <= row

    u = jax.nn.gelu(u_ref[...].astype(F32))
    va = _layernorm_rows(jax.nn.gelu(va_ref[...].astype(F32)), ln_g_ref[...]).astype(BF16)
    za = za_ref[...].astype(F32)
    for g in range(A_GROUPS):
        cols = slice(g * CHUNK, (g + 1) * CHUNK)
        w = jnp.where(causal, ws_ref[g], jnp.zeros((), BF16))
        sg = _dot(w, va[:, cols]) + bs_ref[:, g:g + 1]
        o_ref[:, cols] = (u[:, cols] * sg * _silu(za[:, cols])).astype(BF16)

    xm = xm_ref[...].astype(F32)
    prev = prev_ref[...]
    prev_ref[...] = xm
    row_w = lax.broadcasted_iota(jnp.int32, (CHUNK, D_MODEL), 0)
    conv = xm * cw_ref[B_CONV - 1:B_CONV, :] + cb_ref[...]
    for j in range(1, B_CONV):
        shifted = jnp.where(row_w < j, pltpu.roll(prev, j, 0), pltpu.roll(xm, j, 0))
        conv = conv + shifted * cw_ref[B_CONV - 1 - j:B_CONV - j, :]
    xc = _silu(conv)
    xc_b = xc.astype(BF16)
    xm_b = xm_ref[...]

    pre = gate_ref[...] + gbias_ref[...]
    logf = jnp.minimum(pre, 0.0) - jnp.log(1.0 + jnp.exp(-jnp.abs(pre)))
    tri = causal.astype(BF16)
    logf_hi = logf.astype(BF16)
    logf_lo = (logf - logf_hi.astype(F32)).astype(BF16)
    bcum = _dot(tri, logf_hi) + _dot(tri, logf_lo)
    pre_t = pre.T
    bcum_t = bcum.T

    og = og_ref[...].astype(F32)
    zb = zb_ref[...].astype(F32)
    for h in range(B_HEADS):
        cols = slice(h * B_HEAD_DIM, (h + 1) * B_HEAD_DIM)
        q = _dot(xc_b[:, cols], wq_ref[h])
        k = _dot(xc_b[:, cols], wk_ref[h]) * (B_HEAD_DIM ** -0.5)
        v_b = _dot(xm_b[:, cols], wv_ref[h]).astype(BF16)
        q_b = q.astype(BF16)

        i_col = pre[:, h:h + 1]
        i_row = pre_t[h:h + 1, :]
        b_col = bcum[:, B_HEADS + h:B_HEADS + h + 1]
        b_row = bcum_t[B_HEADS + h:B_HEADS + h + 1, :]
        b_last = b_col[CHUNK - 1:CHUNK, :]
        m_prev = m_ref[h][:, 0:1]
        c_prev = c_ref[h]
        n_prev = n_ref[h]

        a_col = b_col + m_prev
        dmat = jnp.where(causal, b_col - b_row + i_row, -jnp.inf)
        m_t = jnp.maximum(a_col, jnp.max(dmat, axis=1, keepdims=True))
        sc = _dot_nt(q_b, k.astype(BF16)) * jnp.exp(dmat - m_t)
        w_inter = jnp.exp(a_col - m_t)
        num = _dot(sc.astype(BF16), v_b) + w_inter * _dot(q_b, c_prev.astype(BF16))
        den = (jnp.sum(sc, axis=1, keepdims=True)
               + w_inter * jnp.sum(q * n_prev, axis=1, keepdims=True))
        h_t = num / jnp.maximum(jnp.abs(den), jnp.exp(-m_t))

        g_col = b_last - b_col + i_col
        g_row = b_last - b_row + i_row
        m_new = jnp.maximum(b_last + m_prev, jnp.max(g_row, axis=1, keepdims=True))
        kw = k * jnp.exp(g_col - m_new)
        decay = jnp.exp(b_last + m_prev - m_new)
        c_ref[h] = decay * c_prev + _dot_tn(kw.astype(BF16), v_b)
        n_ref[h] = decay * n_prev + jnp.sum(kw, axis=0, keepdims=True)
        m_ref[h] = jnp.broadcast_to(m_new, (1, LANES))

        h_t = _layernorm_rows(h_t * _sigmoid(og[:, cols]), gn_g_ref[:, cols])
        y_b = (h_t + skip_ref[:, cols] * xc[:, cols]) * _silu(zb[:, cols])
        o_ref[:, D_MODEL + h * B_HEAD_DIM:D_MODEL + (h + 1) * B_HEAD_DIM] = y_b.astype(BF16)


def _even_mixer(proj, gates, ln_g, ws, bs_t, cw, cb, wq, wk, wv, gbias, gn_g, skip, batch, seq):
    t = proj.shape[0]
    nc = seq // CHUNK

    def col_block(j):
        return pl.BlockSpec((CHUNK, D_MODEL), lambda b, c, j=j: (b * nc + c, j))

    def whole(a):
        nd = a.ndim
        return pl.BlockSpec(a.shape, lambda b, c, nd=nd: (0,) * nd)

    consts = (ln_g, ws, bs_t, cw, cb, wq, wk, wv, gbias, gn_g, skip)
    return pl.pallas_call(
        _even_mixer_kernel,
        grid=(batch, nc),
        in_specs=[col_block(j) for j in range(6)]
        + [pl.BlockSpec((CHUNK, LANES), lambda b, c: (b * nc + c, 0))]
        + [whole(a) for a in consts],
        out_specs=pl.BlockSpec((CHUNK, 2 * D_MODEL), lambda b, c: (b * nc + c, 0)),
        out_shape=jax.ShapeDtypeStruct((t, 2 * D_MODEL), BF16),
        scratch_shapes=[
            pltpu.VMEM((CHUNK, D_MODEL), F32),
            pltpu.VMEM((B_HEADS, B_HEAD_DIM, B_HEAD_DIM), F32),
            pltpu.VMEM((B_HEADS, 1, B_HEAD_DIM), F32),
            pltpu.VMEM((B_HEADS, 1, LANES), F32),
        ],
        compiler_params=_params(("parallel", "arbitrary")),
        name="even_mixer",
    )(proj, proj, proj, proj, proj, proj, gates, *consts)


ATTN_TILE = 512
ATTN_ROWS = 64
ATTN_UNROLL = 4


def _attn_kernel(lamp_ref, subln_ref, q_ref, k_ref, v_ref, z_ref, o_ref,
                 vext_ref, qc_ref, p0_ref, p1_ref, alpha0_ref, alpha1_ref, m_ref, acc_ref,
                 *, tile, lam_init):
    p_refs, alpha_refs = (p0_ref, p1_ref), (alpha0_ref, alpha1_ref)
    seq = q_ref.shape[0]
    nq = seq // tile
    rows = min(ATTN_ROWS, tile)

    lane = lax.broadcasted_iota(jnp.int32, (seq, LANES), 1)
    zero = jnp.zeros((), BF16)

    vext_ref[:, :LANES] = v_ref[...]
    vext_ref[:, LANES:] = jnp.ones((seq, LANES), BF16)
    q_all = q_ref[...]
    qc_ref[0] = jnp.where(lane < C_HEAD_DIM, q_all, zero)
    qc_ref[1] = jnp.where(lane >= C_HEAD_DIM, q_all, zero)

    def start(i):
        return pl.multiple_of(jnp.asarray(i, jnp.int32) * tile, tile)

    def softmax_stage(qi, j, slot, diagonal):
        k = k_ref[pl.ds(start(j), tile), :]
        for c in range(2):
            s_all = _dot_nt(qc_ref[c, pl.ds(start(qi), tile), :], k)
            for r in range(0, tile, rows):
                s = s_all[r:r + rows, :]
                if diagonal:
                    keep = (lax.broadcasted_iota(jnp.int32, (rows, tile), 1)
                            <= lax.broadcasted_iota(jnp.int32, (rows, tile), 0) + r)
                    s = jnp.where(keep, s, -jnp.inf)
                    m_new = jnp.broadcast_to(jnp.max(s, axis=1, keepdims=True), (rows, LANES))
                else:
                    m_prev = m_ref[qi, c, r:r + rows, :]
                    m_new = jnp.maximum(m_prev, jnp.max(s, axis=1, keepdims=True))
                    alpha_refs[slot][c, r:r + rows, :] = jnp.exp2(m_prev - m_new)
                p = jnp.exp2(s - jnp.concatenate([m_new] * (tile // LANES), axis=1))
                p_refs[slot][c, r:r + rows, :] = p.astype(BF16)
                m_ref[qi, c, r:r + rows, :] = m_new

    def values_stage(qi, j, slot, diagonal):
        vext = vext_ref[pl.ds(start(j), tile), :]
        for c in range(2):
            pv = _dot(p_refs[slot][c], vext)
            if diagonal:
                acc_ref[qi, c] = pv
            else:
                alpha = alpha_refs[slot][c]
                acc_ref[qi, c] = jnp.concatenate([alpha, alpha], axis=1) * acc_ref[qi, c] + pv

    def pipeline(blocks, diagonal, unroll):
        nblk = len(blocks)

        def static_step(t):
            if t < nblk:
                softmax_stage(*blocks[t], t % 2, diagonal)
            if 1 <= t <= nblk:
                values_stage(*blocks[t - 1], (t - 1) % 2, diagonal)

        def next_block(qi, j):
            if diagonal:
                return qi + 1, j + 1
            wrap = j + 1 == qi
            return jnp.where(wrap, qi + 1, qi), jnp.where(wrap, 0, j + 1)

        static_step(0)
        n_trips = max(nblk - 1, 0) // unroll
        if n_trips > 0:
            def body(i, carry):
                prev, cur = carry
                for u in range(unroll):
                    softmax_stage(*cur, (1 + u) % 2, diagonal)
                    values_stage(*prev, u % 2, diagonal)
                    prev, cur = cur, next_block(*cur)
                return prev, cur

            as_i32 = lambda b: (jnp.int32(b[0]), jnp.int32(b[1]))
            lax.fori_loop(0, n_trips, body, (as_i32(blocks[0]), as_i32(blocks[1])))
        for t in range(1 + unroll * n_trips, nblk + 1):
            static_step(t)

    pipeline([(qi, qi) for qi in range(nq)], True, 2)
    if nq > 1:
        pipeline([(qi, j) for qi in range(1, nq) for j in range(qi)], False, ATTN_UNROLL)

    lp = lamp_ref[...]
    lam = (jnp.exp(jnp.sum(lp[0:1] * lp[1:2], axis=-1, keepdims=True))
           - jnp.exp(jnp.sum(lp[2:3] * lp[3:4], axis=-1, keepdims=True)) + lam_init)

    def finalize(qi, carry):
        a0, a1 = acc_ref[qi, 0], acc_ref[qi, 1]
        o = a0[:, :LANES] / a0[:, LANES:] - lam * (a1[:, :LANES] / a1[:, LANES:])
        o = _rmsnorm_rows(o, subln_ref[...]) * (1.0 - lam_init)
        z = z_ref[pl.ds(start(qi), tile), :].astype(F32)
        o_ref[pl.ds(start(qi), tile), :] = (o * _silu(z)).astype(BF16)
        return carry

    lax.fori_loop(0, nq, finalize, 0)


def _attention(proj, lamp, subln, batch, seq, lam_init):
    t = proj.shape[0]
    tile = min(ATTN_TILE, seq)
    nq = seq // tile

    def head_block(part):
        return pl.BlockSpec((seq, LANES), lambda b, h, part=part: (b, part * C_HEADS + h))

    slots = lambda shape, dtype: [pltpu.VMEM(shape, dtype), pltpu.VMEM(shape, dtype)]
    return pl.pallas_call(
        functools.partial(_attn_kernel, tile=tile, lam_init=lam_init),
        grid=(batch, C_HEADS),
        in_specs=[
            pl.BlockSpec(lamp.shape, lambda b, h: (0, 0)),
            pl.BlockSpec(subln.shape, lambda b, h: (0, 0)),
            head_block(0), head_block(1), head_block(2), head_block(3),
        ],
        out_specs=pl.BlockSpec((seq, LANES), lambda b, h: (b, h)),
        out_shape=jax.ShapeDtypeStruct((t, C_HEADS * LANES), BF16),
        scratch_shapes=[pltpu.VMEM((seq, 2 * LANES), BF16),
                        pltpu.VMEM((2, seq, LANES), BF16)]
        + slots((2, tile, tile), BF16)
        + slots((2, tile, LANES), F32)
        + [pltpu.VMEM((nq, 2, tile, LANES), F32),
           pltpu.VMEM((nq, 2, tile, 2 * LANES), F32)],
        compiler_params=_params(("parallel", "parallel")),
        name="diff_attention",
    )(lamp, subln, proj, proj, proj, proj)


def _lambda_init(layer):
    return 0.8 - 0.6 * math.exp(-0.3 * layer)


def _rope_tables(seq):
    half = C_HEAD_DIM // 2
    inv = ROPE_THETA ** (-jnp.arange(0, C_HEAD_DIM, 2, dtype=F32) / C_HEAD_DIM)
    ang = jnp.arange(seq, dtype=F32)[:, None] * inv[None, :]
    cos = jnp.tile(jnp.cos(ang), (1, LANES // half))
    sin = jnp.tile(jnp.sin(ang), (1, LANES // half))
    first_half = (jnp.arange(LANES) % C_HEAD_DIM) < half
    return cos, jnp.where(first_half, -sin, 0.0), jnp.where(first_half, 0.0, sin)


def kernel(x, norm_g, ev_w_in, ev_w_out, a_ln_g, a_ws, a_bs, b_conv_w, b_conv_b, b_wq, b_wk, b_wv,
           b_ig_b, b_fg_b, b_gn_g, b_skip, od_w_in, od_w_out, c_lam_q1, c_lam_k1, c_lam_q2,
           c_lam_k2, c_subln_g, final_g):
    batch, seq, d = x.shape
    depth = norm_g.shape[0]
    assert d == D_MODEL and seq % PROJ_TM == 0 and (batch * seq) % OUT_TM == 0
    x2 = x.reshape(batch * seq, d)
    cos, sin_lo, sin_hi = _rope_tables(seq)
    fg = final_g.reshape(1, d)
    row = lambda a: a.reshape(1, -1)
    for layer in range(depth):
        g = row(norm_g[layer])
        final = layer == depth - 1
        if layer % 2 == 0:
            e = layer // 2
            w_in = ev_w_in[e]
            w_main = jnp.concatenate([w_in[:, :GATE_LO], w_in[:, GATE_LO + 2 * B_HEADS:]], axis=1).astype(BF16)
            w_gate = jnp.pad(w_in[:, GATE_LO:GATE_LO + 2 * B_HEADS], ((0, 0), (0, LANES - 2 * B_HEADS))).astype(BF16)
            proj, gates = _even_proj(x2, g, w_main, w_gate)
            gbias = jnp.pad(jnp.concatenate([b_ig_b[e], b_fg_b[e]]), (0, LANES - 2 * B_HEADS)).reshape(1, LANES)
            y = _even_mixer(proj, gates, row(a_ln_g[e]), a_ws[e].astype(BF16), a_bs[e].T, b_conv_w[e],
                            row(b_conv_b[e]), b_wq[e].astype(BF16), b_wk[e].astype(BF16),
                            b_wv[e].astype(BF16), gbias, row(b_gn_g[e]), row(b_skip[e]), batch, seq)
            x2 = _out_proj(y, ev_w_out[e].astype(BF16), x2, fg, final)
        else:
            o = layer // 2
            proj = _odd_proj(x2, g, od_w_in[o].astype(BF16), cos, sin_lo, sin_hi, seq)
            lamp = jnp.stack([c_lam_q1[o], c_lam_k1[o], c_lam_q2[o], c_lam_k2[o]])
            y = _attention(proj, lamp, row(c_subln_g[o]), batch, seq, _lambda_init(layer))
            x2 = _out_proj(y, od_w_out[o].astype(BF16), x2, fg, final)
    return x2.reshape(batch, seq, d)
```

```python
import functools
import math

import jax
import jax.numpy as jnp
from jax import lax
from jax.experimental import pallas as pl
from jax.experimental.pallas import tpu as pltpu

F32 = jnp.float32
BF16 = jnp.bfloat16

D_MODEL = 1024
A_GROUPS = 8
CHUNK = 128
B_HEADS = 4
B_HEAD_DIM = 256
B_CONV = 4
C_HEADS = 16
C_HEAD_DIM = 64
LANES = 128
ROPE_THETA = 10000.0
NORM_EPS = 1e-6
LOG2E = 1.4426950408889634
EVEN_MAIN = 6 * D_MODEL
ODD_IN = 8 * D_MODEL
GATE_LO = 5 * D_MODEL

VMEM_LIMIT_BYTES = 56 * 1024 * 1024


def _params(semantics):
    return pltpu.CompilerParams(dimension_semantics=semantics, vmem_limit_bytes=VMEM_LIMIT_BYTES)


def _rmsnorm_rows(x, g):
    return x * lax.rsqrt(jnp.mean(x * x, axis=-1, keepdims=True) + NORM_EPS) * g


def _layernorm_rows(x, g):
    xc = x - jnp.mean(x, axis=-1, keepdims=True)
    return xc * lax.rsqrt(jnp.mean(xc * xc, axis=-1, keepdims=True) + NORM_EPS) * g


def _sigmoid(x):
    return 1.0 / (1.0 + jnp.exp(-x))


def _silu(x):
    return x * _sigmoid(x)


def _dot(a, b):
    return jnp.dot(a, b, preferred_element_type=F32)


def _dot_nt(a, b):
    return lax.dot_general(a, b, (((1,), (1,)), ((), ())), preferred_element_type=F32)


def _dot_tn(a, b):
    return lax.dot_general(a, b, (((0,), (0,)), ((), ())), preferred_element_type=F32)


PROJ_TM = 512
PROJ_TN = 512


def _even_proj_kernel(x_ref, g_ref, w_ref, wg_ref, o_ref, gate_ref):
    hn = _rmsnorm_rows(x_ref[...], g_ref[...]).astype(BF16)
    gate_ref[...] = _dot(hn, wg_ref[...])
    for c in range(EVEN_MAIN // PROJ_TN):
        cols = slice(c * PROJ_TN, (c + 1) * PROJ_TN)
        o_ref[:, cols] = _dot(hn, w_ref[:, cols]).astype(BF16)


def _odd_proj_kernel(x_ref, g_ref, w_ref, cos_ref, sin_lo_ref, sin_hi_ref, o_ref):
    hn = _rmsnorm_rows(x_ref[...], g_ref[...]).astype(BF16)
    cos, sin_lo, sin_hi = cos_ref[...], sin_lo_ref[...], sin_hi_ref[...]
    q_cols = C_HEADS * 2 * C_HEAD_DIM
    for c in range(ODD_IN // PROJ_TN):
        cols = slice(c * PROJ_TN, (c + 1) * PROJ_TN)
        acc = _dot(hn, w_ref[:, cols])
        if c * PROJ_TN < 2 * q_cols:
            scale = C_HEAD_DIM ** -0.5 * LOG2E if c * PROJ_TN < q_cols else 1.0
            for j in range(PROJ_TN // LANES):
                blk = acc[:, j * LANES:(j + 1) * LANES]
                rot = (blk * cos + pltpu.roll(blk, LANES - C_HEAD_DIM // 2, 1) * sin_lo
                       + pltpu.roll(blk, C_HEAD_DIM // 2, 1) * sin_hi)
                o_ref[:, c * PROJ_TN + j * LANES:c * PROJ_TN + (j + 1) * LANES] = (rot * scale).astype(BF16)
        else:
            o_ref[:, cols] = acc.astype(BF16)


def _even_proj(x2, g, w_main, w_gate):
    t = x2.shape[0]
    return pl.pallas_call(
        _even_proj_kernel,
        grid=(t // PROJ_TM,),
        in_specs=[
            pl.BlockSpec((PROJ_TM, D_MODEL), lambda i: (i, 0)),
            pl.BlockSpec((1, D_MODEL), lambda i: (0, 0)),
            pl.BlockSpec((D_MODEL, EVEN_MAIN), lambda i: (0, 0), pipeline_mode=pl.Buffered(1)),
            pl.BlockSpec((D_MODEL, LANES), lambda i: (0, 0)),
        ],
        out_specs=[
            pl.BlockSpec((PROJ_TM, EVEN_MAIN), lambda i: (i, 0)),
            pl.BlockSpec((PROJ_TM, LANES), lambda i: (i, 0)),
        ],
        out_shape=[
            jax.ShapeDtypeStruct((t, EVEN_MAIN), BF16),
            jax.ShapeDtypeStruct((t, LANES), F32),
        ],
        compiler_params=_params(("parallel",)),
        name="even_proj",
    )(x2, g, w_main, w_gate)


def _odd_proj(x2, g, w, cos, sin_lo, sin_hi, seq):
    t = x2.shape[0]
    pos_blocks = seq // PROJ_TM
    tab = pl.BlockSpec((PROJ_TM, LANES), lambda i: (i % pos_blocks, 0))
    return pl.pallas_call(
        _odd_proj_kernel,
        grid=(t // PROJ_TM,),
        in_specs=[
            pl.BlockSpec((PROJ_TM, D_MODEL), lambda i: (i, 0)),
            pl.BlockSpec((1, D_MODEL), lambda i: (0, 0)),
            pl.BlockSpec((D_MODEL, ODD_IN), lambda i: (0, 0), pipeline_mode=pl.Buffered(1)),
            tab, tab, tab,
        ],
        out_specs=pl.BlockSpec((PROJ_TM, ODD_IN), lambda i: (i, 0)),
        out_shape=jax.ShapeDtypeStruct((t, ODD_IN), BF16),
        compiler_params=_params(("parallel",)),
        name="odd_proj",
    )(x2, g, w, cos, sin_lo, sin_hi)


OUT_TM = 512


def _out_proj_kernel(y_ref, w_ref, x_ref, fg_ref, o_ref, *, final):
    acc = _dot(y_ref[...], w_ref[...]) + x_ref[...]
    if final:
        acc = _rmsnorm_rows(acc, fg_ref[...])
    o_ref[...] = acc


def _out_proj(y, w, x2, final_g, final):
    t, k = y.shape
    return pl.pallas_call(
        functools.partial(_out_proj_kernel, final=final),
        grid=(t // OUT_TM,),
        in_specs=[
            pl.BlockSpec((OUT_TM, k), lambda i: (i, 0)),
            pl.BlockSpec((k, D_MODEL), lambda i: (0, 0)),
            pl.BlockSpec((OUT_TM, D_MODEL), lambda i: (i, 0)),
            pl.BlockSpec((1, D_MODEL), lambda i: (0, 0)),
        ],
        out_specs=pl.BlockSpec((OUT_TM, D_MODEL), lambda i: (i, 0)),
        out_shape=jax.ShapeDtypeStruct((t, D_MODEL), F32),
        compiler_params=_params(("parallel",)),
        name="out_proj_final" if final else "out_proj",
    )(y, w, x2, final_g)


MIX_BATCH = 2
GELU_C0 = math.sqrt(2.0 / math.pi)
GELU_C1 = 0.044715 * GELU_C0


def _gelu_tanh(x):
    hx = 0.5 * x
    return hx + hx * jnp.tanh(x * (GELU_C0 + GELU_C1 * (x * x)))


def _silu_tanh(x):
    hx = 0.5 * x
    return hx + hx * jnp.tanh(hx)


def _even_mixer_kernel(u_ref, va_ref, za_ref, xm_ref, og_ref, zb_ref, gate_ref,
                       ln_g_ref, ws_ref, bs_ref, cw_ref, cb_ref, wq_ref, wk_ref, wv_ref,
                       gbias_ref, gn_g_ref, skip_ref,
                       o_ref, prev_ref, c_ref, n_ref, m_ref):
    @pl.when(pl.program_id(1) == 0)
    def _():
        prev_ref[...] = jnp.zeros_like(prev_ref)
        c_ref[...] = jnp.zeros_like(c_ref)
        n_ref[...] = jnp.zeros_like(n_ref)
        m_ref[...] = jnp.zeros_like(m_ref)

    seqs = range(u_ref.shape[0])
    row = lax.broadcasted_iota(jnp.int32, (CHUNK, CHUNK), 0)
    col = lax.broadcasted_iota(jnp.int32, (CHUNK, CHUNK), 1)
    causal = col <= row

    w_s = [jnp.where(causal, ws_ref[g], jnp.zeros((), BF16)) for g in range(A_GROUPS)]
    for b in seqs:
        u = _gelu_tanh(u_ref[b].astype(F32))
        va = _layernorm_rows(_gelu_tanh(va_ref[b].astype(F32)), ln_g_ref[...]).astype(BF16)
        za = za_ref[b].astype(F32)
        for g in range(A_GROUPS):
            cols = slice(g * CHUNK, (g + 1) * CHUNK)
            sg = _dot(w_s[g], va[:, cols]) + bs_ref[:, g:g + 1]
            o_ref[b, :, cols] = (u[:, cols] * sg * _silu_tanh(za[:, cols])).astype(BF16)

    sel_row = lax.broadcasted_iota(jnp.int32, ((B_CONV - 1) * CHUNK, 2 * CHUNK), 0)
    sel_col = lax.broadcasted_iota(jnp.int32, ((B_CONV - 1) * CHUNK, 2 * CHUNK), 1)
    chunk_bits = CHUNK.bit_length() - 1
    shift = jnp.right_shift(sel_row, chunk_bits) + 1
    select = (sel_col == jnp.bitwise_and(sel_row, CHUNK - 1) + CHUNK - shift).astype(BF16)
    tri = causal.astype(BF16)
    xc, xc_b, xm_b, pre, pre_t, bcum, bcum_t = [], [], [], [], [], [], []
    for b in seqs:
        xm_b.append(xm_ref[b])
        window = jnp.concatenate([prev_ref[b], xm_b[b]], axis=0)
        prev_ref[b] = xm_b[b]
        shifted = _dot(select, window)
        conv = xm_b[b].astype(F32) * cw_ref[B_CONV - 1:B_CONV, :] + cb_ref[...]
        for j in range(1, B_CONV):
            conv = conv + shifted[(j - 1) * CHUNK:j * CHUNK, :] * cw_ref[B_CONV - 1 - j:B_CONV - j, :]
        xc.append(_silu_tanh(conv))
        xc_b.append(xc[b].astype(BF16))

        pre.append(gate_ref[b] + gbias_ref[...])
        logf = jnp.minimum(pre[b], 0.0) - jnp.log(1.0 + jnp.exp(-jnp.abs(pre[b])))
        logf_hi = logf.astype(BF16)
        logf_lo = (logf - logf_hi.astype(F32)).astype(BF16)
        bcum.append(_dot(tri, logf_hi) + _dot(tri, logf_lo))
        pre_t.append(pre[b].T)
        bcum_t.append(bcum[b].T)

    units = [(b, h) for b in seqs for h in range(B_HEADS)]
    hcols = [slice(h * B_HEAD_DIM, (h + 1) * B_HEAD_DIM) for h in range(B_HEADS)]
    q = [_dot(xc_b[b][:, hcols[h]], wq_ref[h]) for b, h in units]
    k = [_dot(xc_b[b][:, hcols[h]], wk_ref[h]) * (B_HEAD_DIM ** -0.5) for b, h in units]
    v_b = [_dot(xm_b[b][:, hcols[h]], wv_ref[h]).astype(BF16) for b, h in units]
    idx = range(len(units))
    q_b = [q[i].astype(BF16) for i in idx]

    i_col = [pre[b][:, h:h + 1] for b, h in units]
    i_row = [pre_t[b][h:h + 1, :] for b, h in units]
    b_col = [bcum[b][:, B_HEADS + h:B_HEADS + h + 1] for b, h in units]
    b_row = [bcum_t[b][B_HEADS + h:B_HEADS + h + 1, :] for b, h in units]
    b_last = [b_col[i][CHUNK - 1:CHUNK, :] for i in idx]
    m_prev = [m_ref[b, h][:, 0:1] for b, h in units]
    c_prev = [c_ref[b, h] for b, h in units]
    n_prev = [n_ref[b, h] for b, h in units]

    a_col = [b_col[i] + m_prev[i] for i in idx]
    dmat = [jnp.where(causal, b_col[i] - b_row[i] + i_row[i], -jnp.inf) for i in idx]
    m_t = [jnp.maximum(a_col[i], jnp.max(dmat[i], axis=1, keepdims=True)) for i in idx]
    sc = [_dot_nt(q_b[i], k[i].astype(BF16)) * jnp.exp(dmat[i] - m_t[i]) for i in idx]
    w_inter = [jnp.exp(a_col[i] - m_t[i]) for i in idx]
    q_c = [_dot(q_b[i], c_prev[i].astype(BF16)) for i in idx]
    num = [_dot(sc[i].astype(BF16), v_b[i]) + w_inter[i] * q_c[i] for i in idx]
    den = [jnp.sum(sc[i], axis=1, keepdims=True)
           + w_inter[i] * jnp.sum(q[i] * n_prev[i], axis=1, keepdims=True) for i in idx]
    h_t = [num[i] / jnp.maximum(jnp.abs(den[i]), jnp.exp(-m_t[i])) for i in idx]

    g_col = [b_last[i] - b_col[i] + i_col[i] for i in idx]
    g_row = [b_last[i] - b_row[i] + i_row[i] for i in idx]
    m_new = [jnp.maximum(b_last[i] + m_prev[i], jnp.max(g_row[i], axis=1, keepdims=True)) for i in idx]
    kw = [k[i] * jnp.exp(g_col[i] - m_new[i]) for i in idx]
    decay = [jnp.exp(b_last[i] + m_prev[i] - m_new[i]) for i in idx]
    for i, (b, h) in enumerate(units):
        c_ref[b, h] = decay[i] * c_prev[i] + _dot_tn(kw[i].astype(BF16), v_b[i])
    for i, (b, h) in enumerate(units):
        n_ref[b, h] = decay[i] * n_prev[i] + jnp.sum(kw[i], axis=0, keepdims=True)
        m_ref[b, h] = jnp.broadcast_to(m_new[i], (1, LANES))

    for i, (b, h) in enumerate(units):
        og = og_ref[b, :, hcols[h]].astype(F32)
        zb = zb_ref[b, :, hcols[h]].astype(F32)
        gated = h_t[i] * (0.5 + 0.5 * jnp.tanh(0.5 * og))
        h_n = _layernorm_rows(gated, gn_g_ref[:, hcols[h]])
        y_b = (h_n + skip_ref[:, hcols[h]] * xc[b][:, hcols[h]]) * _silu_tanh(zb)
        o_ref[b, :, D_MODEL + h * B_HEAD_DIM:D_MODEL + (h + 1) * B_HEAD_DIM] = y_b.astype(BF16)


def _even_mixer(proj, gates, ln_g, ws, bs_t, cw, cb, wq, wk, wv, gbias, gn_g, skip, batch, seq):
    t = proj.shape[0]
    nc = seq // CHUNK
    mb = MIX_BATCH if batch % MIX_BATCH == 0 else 1
    proj3 = proj.reshape(batch, seq, EVEN_MAIN)
    gates3 = gates.reshape(batch, seq, LANES)

    def col_block(j):
        return pl.BlockSpec((mb, CHUNK, D_MODEL), lambda b, c, j=j: (b, c, j))

    def whole(a):
        nd = a.ndim
        return pl.BlockSpec(a.shape, lambda b, c, nd=nd: (0,) * nd)

    consts = (ln_g, ws, bs_t, cw, cb, wq, wk, wv, gbias, gn_g, skip)
    out = pl.pallas_call(
        _even_mixer_kernel,
        grid=(batch // mb, nc),
        in_specs=[col_block(j) for j in range(6)]
        + [pl.BlockSpec((mb, CHUNK, LANES), lambda b, c: (b, c, 0))]
        + [whole(a) for a in consts],
        out_specs=pl.BlockSpec((mb, CHUNK, 2 * D_MODEL), lambda b, c: (b, c, 0)),
        out_shape=jax.ShapeDtypeStruct((batch, seq, 2 * D_MODEL), BF16),
        scratch_shapes=[
            pltpu.VMEM((mb, CHUNK, D_MODEL), BF16),
            pltpu.VMEM((mb, B_HEADS, B_HEAD_DIM, B_HEAD_DIM), F32),
            pltpu.VMEM((mb, B_HEADS, 1, B_HEAD_DIM), F32),
            pltpu.VMEM((mb, B_HEADS, 1, LANES), F32),
        ],
        compiler_params=_params(("parallel", "arbitrary")),
        name="even_mixer",
    )(proj3, proj3, proj3, proj3, proj3, proj3, gates3, *consts)
    return out.reshape(t, 2 * D_MODEL)


ATTN_TILE = 512
ATTN_ROWS = 64
ATTN_BAND = 256
ATTN_UNROLL = 4


def _attn_kernel(lamp_ref, subln_ref, q_ref, k_ref, v_ref, z_ref, o_ref,
                 vext_ref, qc_ref, p0_ref, p1_ref, alpha0_ref, alpha1_ref, m_ref, acc_ref,
                 *, tile, lam_init):
    p_refs, alpha_refs = (p0_ref, p1_ref), (alpha0_ref, alpha1_ref)
    seq = q_ref.shape[0]
    nq = seq // tile
    rows = min(ATTN_ROWS, tile)
    half = min(ATTN_BAND, tile)

    lane = lax.broadcasted_iota(jnp.int32, (seq, LANES), 1)
    zero = jnp.zeros((), BF16)

    vext_ref[:, :LANES] = v_ref[...]
    vext_ref[:, LANES:] = jnp.ones((seq, LANES), BF16)
    q_all = q_ref[...]
    qc_ref[0] = jnp.where(lane < C_HEAD_DIM, q_all, zero)
    qc_ref[1] = jnp.where(lane >= C_HEAD_DIM, q_all, zero)

    def start(i):
        return pl.multiple_of(jnp.asarray(i, jnp.int32) * tile, tile)

    bands = [(r0, r0 + half) for r0 in range(0, tile, half)]

    def softmax_stage(qi, j, slot, diagonal):
        if diagonal:
            for c in range(2):
                for r0, keys in bands:
                    k = k_ref[pl.ds(start(j), keys), :]
                    s_all = _dot_nt(qc_ref[c, pl.ds(start(qi) + r0, half), :], k)
                    for r in range(0, half, rows):
                        keep = (lax.broadcasted_iota(jnp.int32, (rows, keys), 1)
                                <= lax.broadcasted_iota(jnp.int32, (rows, keys), 0) + (r0 + r))
                        s = jnp.where(keep, s_all[r:r + rows, :], -jnp.inf)
                        m_new = jnp.broadcast_to(jnp.max(s, axis=1, keepdims=True), (rows, LANES))
                        p = jnp.exp2(s - jnp.concatenate([m_new] * (keys // LANES), axis=1))
                        p_refs[slot][c, r0 + r:r0 + r + rows, :keys] = p.astype(BF16)
                        m_ref[qi, c, r0 + r:r0 + r + rows, :] = m_new
            return
        k = k_ref[pl.ds(start(j), tile), :]
        for c in range(2):
            s_all = _dot_nt(qc_ref[c, pl.ds(start(qi), tile), :], k)
            for r in range(0, tile, rows):
                s = s_all[r:r + rows, :]
                m_prev = m_ref[qi, c, r:r + rows, :]
                m_new = jnp.maximum(m_prev, jnp.max(s, axis=1, keepdims=True))
                alpha_refs[slot][c, r:r + rows, :] = jnp.exp2(m_prev - m_new)
                p = jnp.exp2(s - jnp.concatenate([m_new] * (tile // LANES), axis=1))
                p_refs[slot][c, r:r + rows, :] = p.astype(BF16)
                m_ref[qi, c, r:r + rows, :] = m_new

    def values_stage(qi, j, slot, diagonal):
        if diagonal:
            for c in range(2):
                for r0, keys in bands:
                    vext = vext_ref[pl.ds(start(j), keys), :]
                    acc_ref[qi, c, r0:r0 + half, :] = _dot(p_refs[slot][c, r0:r0 + half, :keys], vext)
            return
        vext = vext_ref[pl.ds(start(j), tile), :]
        for c in range(2):
            alpha = alpha_refs[slot][c]
            acc_ref[qi, c] = (jnp.concatenate([alpha, alpha], axis=1) * acc_ref[qi, c]
                              + _dot(p_refs[slot][c], vext))

    def pipeline(blocks, diagonal, unroll):
        nblk = len(blocks)

        def static_step(t):
            if t < nblk:
                softmax_stage(*blocks[t], t % 2, diagonal)
            if 1 <= t <= nblk:
                values_stage(*blocks[t - 1], (t - 1) % 2, diagonal)

        def next_block(qi, j):
            if diagonal:
                return qi + 1, j + 1
            wrap = j + 1 == qi
            return jnp.where(wrap, qi + 1, qi), jnp.where(wrap, 0, j + 1)

        static_step(0)
        n_trips = max(nblk - 1, 0) // unroll
        if n_trips > 0:
            def body(i, carry):
                prev, cur = carry
                for u in range(unroll):
                    softmax_stage(*cur, (1 + u) % 2, diagonal)
                    values_stage(*prev, u % 2, diagonal)
                    prev, cur = cur, next_block(*cur)
                return prev, cur

            as_i32 = lambda b: (jnp.int32(b[0]), jnp.int32(b[1]))
            lax.fori_loop(0, n_trips, body, (as_i32(blocks[0]), as_i32(blocks[1])))
        for t in range(1 + unroll * n_trips, nblk + 1):
            static_step(t)

    pipeline([(qi, qi) for qi in range(nq)], True, 2)
    if nq > 1:
        pipeline([(qi, j) for qi in range(1, nq) for j in range(qi)], False, ATTN_UNROLL)

    lp = lamp_ref[...]
    lam = (jnp.exp(jnp.sum(lp[0:1] * lp[1:2], axis=-1, keepdims=True))
           - jnp.exp(jnp.sum(lp[2:3] * lp[3:4], axis=-1, keepdims=True)) + lam_init)

    def finalize(qi, carry):
        a0, a1 = acc_ref[qi, 0], acc_ref[qi, 1]
        o = a0[:, :LANES] / a0[:, LANES:] - lam * (a1[:, :LANES] / a1[:, LANES:])
        o = _rmsnorm_rows(o, subln_ref[...]) * (1.0 - lam_init)
        z = z_ref[pl.ds(start(qi), tile), :].astype(F32)
        o_ref[pl.ds(start(qi), tile), :] = (o * _silu(z)).astype(BF16)
        return carry

    lax.fori_loop(0, nq, finalize, 0)


def _attention(proj, lamp, subln, batch, seq, lam_init):
    t = proj.shape[0]
    tile = min(ATTN_TILE, seq)
    nq = seq // tile

    def head_block(part):
        return pl.BlockSpec((seq, LANES), lambda b, h, part=part: (b, part * C_HEADS + h))

    slots = lambda shape, dtype: [pltpu.VMEM(shape, dtype), pltpu.VMEM(shape, dtype)]
    return pl.pallas_call(
        functools.partial(_attn_kernel, tile=tile, lam_init=lam_init),
        grid=(batch, C_HEADS),
        in_specs=[
            pl.BlockSpec(lamp.shape, lambda b, h: (0, 0)),
            pl.BlockSpec(subln.shape, lambda b, h: (0, 0)),
            head_block(0), head_block(1), head_block(2), head_block(3),
        ],
        out_specs=pl.BlockSpec((seq, LANES), lambda b, h: (b, h)),
        out_shape=jax.ShapeDtypeStruct((t, C_HEADS * LANES), BF16),
        scratch_shapes=[pltpu.VMEM((seq, 2 * LANES), BF16),
                        pltpu.VMEM((2, seq, LANES), BF16)]
        + slots((2, tile, tile), BF16)
        + slots((2, tile, LANES), F32)
        + [pltpu.VMEM((nq, 2, tile, LANES), F32),
           pltpu.VMEM((nq, 2, tile, 2 * LANES), F32)],
        compiler_params=_params(("parallel", "parallel")),
        name="diff_attention",
    )(lamp, subln, proj, proj, proj, proj)


def _lambda_init(layer):
    return 0.8 - 0.6 * math.exp(-0.3 * layer)


def _rope_tables(seq):
    half = C_HEAD_DIM // 2
    inv = ROPE_THETA ** (-jnp.arange(0, C_HEAD_DIM, 2, dtype=F32) / C_HEAD_DIM)
    ang = jnp.arange(seq, dtype=F32)[:, None] * inv[None, :]
    cos = jnp.tile(jnp.cos(ang), (1, LANES // half))
    sin = jnp.tile(jnp.sin(ang), (1, LANES // half))
    first_half = (jnp.arange(LANES) % C_HEAD_DIM) < half
    return cos, jnp.where(first_half, -sin, 0.0), jnp.where(first_half, 0.0, sin)


def kernel(x, norm_g, ev_w_in, ev_w_out, a_ln_g, a_ws, a_bs, b_conv_w, b_conv_b, b_wq, b_wk, b_wv,
           b_ig_b, b_fg_b, b_gn_g, b_skip, od_w_in, od_w_out, c_lam_q1, c_lam_k1, c_lam_q2,
           c_lam_k2, c_subln_g, final_g):
    batch, seq, d = x.shape
    depth = norm_g.shape[0]
    assert d == D_MODEL and seq % PROJ_TM == 0 and (batch * seq) % OUT_TM == 0
    x2 = x.reshape(batch * seq, d)
    cos, sin_lo, sin_hi = _rope_tables(seq)
    fg = final_g.reshape(1, d)
    row = lambda a: a.reshape(1, -1)
    for layer in range(depth):
        g = row(norm_g[layer])
        final = layer == depth - 1
        if layer % 2 == 0:
            e = layer // 2
            w_in = ev_w_in[e]
            w_main = jnp.concatenate([w_in[:, :GATE_LO], w_in[:, GATE_LO + 2 * B_HEADS:]], axis=1).astype(BF16)
            w_gate = jnp.pad(w_in[:, GATE_LO:GATE_LO + 2 * B_HEADS], ((0, 0), (0, LANES - 2 * B_HEADS))).astype(BF16)
            proj, gates = _even_proj(x2, g, w_main, w_gate)
            gbias = jnp.pad(jnp.concatenate([b_ig_b[e], b_fg_b[e]]), (0, LANES - 2 * B_HEADS)).reshape(1, LANES)
            y = _even_mixer(proj, gates, row(a_ln_g[e]), a_ws[e].astype(BF16), a_bs[e].T, b_conv_w[e],
                            row(b_conv_b[e]), b_wq[e].astype(BF16), b_wk[e].astype(BF16),
                            b_wv[e].astype(BF16), gbias, row(b_gn_g[e]), row(b_skip[e]), batch, seq)
            x2 = _out_proj(y, ev_w_out[e].astype(BF16), x2, fg, final)
        else:
            o = layer // 2
            proj = _odd_proj(x2, g, od_w_in[o].astype(BF16), cos, sin_lo, sin_hi, seq)
            lamp = jnp.stack([c_lam_q1[o], c_lam_k1[o], c_lam_q2[o], c_lam_k2[o]])
            y = _attention(proj, lamp, row(c_subln_g[o]), batch, seq, _lambda_init(layer))
            x2 = _out_proj(y, od_w_out[o].astype(BF16), x2, fg, final)
    return x2.reshape(batch, seq, d)
```

```python
import functools
import math

import jax
import jax.numpy as jnp
from jax import lax
from jax.experimental import pallas as pl
from jax.experimental.pallas import tpu as pltpu

F32 = jnp.float32
BF16 = jnp.bfloat16

D_MODEL = 1024
A_GROUPS = 8
CHUNK = 128
B_HEADS = 4
B_HEAD_DIM = 256
B_CONV = 4
C_HEADS = 16
C_HEAD_DIM = 64
LANES = 128
ROPE_THETA = 10000.0
NORM_EPS = 1e-6
LOG2E = 1.4426950408889634
EVEN_MAIN = 6 * D_MODEL
ODD_IN = 8 * D_MODEL
GATE_LO = 5 * D_MODEL

VMEM_LIMIT_BYTES = 56 * 1024 * 1024


def _params(semantics):
    return pltpu.CompilerParams(dimension_semantics=semantics, vmem_limit_bytes=VMEM_LIMIT_BYTES)


def _rmsnorm_rows(x, g):
    return x * lax.rsqrt(jnp.mean(x * x, axis=-1, keepdims=True) + NORM_EPS) * g


def _layernorm_rows(x, g):
    xc = x - jnp.mean(x, axis=-1, keepdims=True)
    return xc * lax.rsqrt(jnp.mean(xc * xc, axis=-1, keepdims=True) + NORM_EPS) * g


def _sigmoid(x):
    return 1.0 / (1.0 + jnp.exp(-x))


def _silu(x):
    return x * _sigmoid(x)


def _dot(a, b):
    return jnp.dot(a, b, preferred_element_type=F32)


def _dot_nt(a, b):
    return lax.dot_general(a, b, (((1,), (1,)), ((), ())), preferred_element_type=F32)


def _dot_tn(a, b):
    return lax.dot_general(a, b, (((0,), (0,)), ((), ())), preferred_element_type=F32)


PROJ_TM = 512
PROJ_TN = 512


def _even_proj_kernel(x_ref, g_ref, w_ref, wg_ref, o_ref, gate_ref):
    hn = _rmsnorm_rows(x_ref[...], g_ref[...]).astype(BF16)
    gate_ref[...] = _dot(hn, wg_ref[...])
    for c in range(EVEN_MAIN // PROJ_TN):
        cols = slice(c * PROJ_TN, (c + 1) * PROJ_TN)
        o_ref[:, cols] = _dot(hn, w_ref[:, cols]).astype(BF16)


def _odd_proj_kernel(x_ref, g_ref, w_ref, cos_ref, sin_lo_ref, sin_hi_ref, o_ref):
    hn = _rmsnorm_rows(x_ref[...], g_ref[...]).astype(BF16)
    cos, sin_lo, sin_hi = cos_ref[...], sin_lo_ref[...], sin_hi_ref[...]
    q_cols = C_HEADS * 2 * C_HEAD_DIM
    for c in range(ODD_IN // PROJ_TN):
        cols = slice(c * PROJ_TN, (c + 1) * PROJ_TN)
        acc = _dot(hn, w_ref[:, cols])
        if c * PROJ_TN < 2 * q_cols:
            scale = C_HEAD_DIM ** -0.5 * LOG2E if c * PROJ_TN < q_cols else 1.0
            for j in range(PROJ_TN // LANES):
                blk = acc[:, j * LANES:(j + 1) * LANES]
                rot = (blk * cos + pltpu.roll(blk, LANES - C_HEAD_DIM // 2, 1) * sin_lo
                       + pltpu.roll(blk, C_HEAD_DIM // 2, 1) * sin_hi)
                o_ref[:, c * PROJ_TN + j * LANES:c * PROJ_TN + (j + 1) * LANES] = (rot * scale).astype(BF16)
        else:
            o_ref[:, cols] = acc.astype(BF16)


def _even_proj(x2, g, w_main, w_gate):
    t = x2.shape[0]
    return pl.pallas_call(
        _even_proj_kernel,
        grid=(t // PROJ_TM,),
        in_specs=[
            pl.BlockSpec((PROJ_TM, D_MODEL), lambda i: (i, 0)),
            pl.BlockSpec((1, D_MODEL), lambda i: (0, 0)),
            pl.BlockSpec((D_MODEL, EVEN_MAIN), lambda i: (0, 0), pipeline_mode=pl.Buffered(1)),
            pl.BlockSpec((D_MODEL, LANES), lambda i: (0, 0)),
        ],
        out_specs=[
            pl.BlockSpec((PROJ_TM, EVEN_MAIN), lambda i: (i, 0)),
            pl.BlockSpec((PROJ_TM, LANES), lambda i: (i, 0)),
        ],
        out_shape=[
            jax.ShapeDtypeStruct((t, EVEN_MAIN), BF16),
            jax.ShapeDtypeStruct((t, LANES), F32),
        ],
        compiler_params=_params(("parallel",)),
        name="even_proj",
    )(x2, g, w_main, w_gate)


def _odd_proj(x2, g, w, cos, sin_lo, sin_hi, seq):
    t = x2.shape[0]
    pos_blocks = seq // PROJ_TM
    tab = pl.BlockSpec((PROJ_TM, LANES), lambda i: (i % pos_blocks, 0))
    return pl.pallas_call(
        _odd_proj_kernel,
        grid=(t // PROJ_TM,),
        in_specs=[
            pl.BlockSpec((PROJ_TM, D_MODEL), lambda i: (i, 0)),
            pl.BlockSpec((1, D_MODEL), lambda i: (0, 0)),
            pl.BlockSpec((D_MODEL, ODD_IN), lambda i: (0, 0), pipeline_mode=pl.Buffered(1)),
            tab, tab, tab,
        ],
        out_specs=pl.BlockSpec((PROJ_TM, ODD_IN), lambda i: (i, 0)),
        out_shape=jax.ShapeDtypeStruct((t, ODD_IN), BF16),
        compiler_params=_params(("parallel",)),
        name="odd_proj",
    )(x2, g, w, cos, sin_lo, sin_hi)


OUT_TM = 512


def _out_proj_kernel(y_ref, w_ref, x_ref, fg_ref, o_ref, *, final):
    acc = _dot(y_ref[...], w_ref[...]) + x_ref[...]
    if final:
        acc = _rmsnorm_rows(acc, fg_ref[...])
    o_ref[...] = acc


def _out_proj(y, w, x2, final_g, final):
    t, k = y.shape
    return pl.pallas_call(
        functools.partial(_out_proj_kernel, final=final),
        grid=(t // OUT_TM,),
        in_specs=[
            pl.BlockSpec((OUT_TM, k), lambda i: (i, 0)),
            pl.BlockSpec((k, D_MODEL), lambda i: (0, 0)),
            pl.BlockSpec((OUT_TM, D_MODEL), lambda i: (i, 0)),
            pl.BlockSpec((1, D_MODEL), lambda i: (0, 0)),
        ],
        out_specs=pl.BlockSpec((OUT_TM, D_MODEL), lambda i: (i, 0)),
        out_shape=jax.ShapeDtypeStruct((t, D_MODEL), F32),
        compiler_params=_params(("parallel",)),
        name="out_proj_final" if final else "out_proj",
    )(y, w, x2, final_g)


MIX_BATCH = 4
GELU_C0 = math.sqrt(2.0 / math.pi)
GELU_C1 = 0.044715 * GELU_C0


def _gelu_tanh(x):
    hx = 0.5 * x
    return hx + hx * jnp.tanh(x * (GELU_C0 + GELU_C1 * (x * x)))


def _silu_tanh(x):
    hx = 0.5 * x
    return hx + hx * jnp.tanh(hx)


def _even_mixer_kernel(u_ref, va_ref, za_ref, xm_ref, og_ref, zb_ref, gate_ref,
                       ln_g_ref, ws_ref, bs_ref, cw_ref, cb_ref, wq_ref, wk_ref, wv_ref,
                       gbias_ref, gn_g_ref, skip_ref,
                       o_ref, prev_ref, c_ref, n_ref, m_ref):
    @pl.when(pl.program_id(1) == 0)
    def _():
        prev_ref[...] = jnp.zeros_like(prev_ref)
        c_ref[...] = jnp.zeros_like(c_ref)
        n_ref[...] = jnp.zeros_like(n_ref)
        m_ref[...] = jnp.zeros_like(m_ref)

    seqs = range(u_ref.shape[0])
    row = lax.broadcasted_iota(jnp.int32, (CHUNK, CHUNK), 0)
    col = lax.broadcasted_iota(jnp.int32, (CHUNK, CHUNK), 1)
    causal = col <= row

    w_s = [jnp.where(causal, ws_ref[g], jnp.zeros((), BF16)) for g in range(A_GROUPS)]
    for b in seqs:
        u = _gelu_tanh(u_ref[b].astype(F32))
        va = _layernorm_rows(_gelu_tanh(va_ref[b].astype(F32)), ln_g_ref[...]).astype(BF16)
        za = za_ref[b].astype(F32)
        for g in range(A_GROUPS):
            cols = slice(g * CHUNK, (g + 1) * CHUNK)
            sg = _dot(w_s[g], va[:, cols]) + bs_ref[:, g:g + 1]
            o_ref[b, :, cols] = (u[:, cols] * sg * _silu_tanh(za[:, cols])).astype(BF16)

    sel_row = lax.broadcasted_iota(jnp.int32, ((B_CONV - 1) * CHUNK, 2 * CHUNK), 0)
    sel_col = lax.broadcasted_iota(jnp.int32, ((B_CONV - 1) * CHUNK, 2 * CHUNK), 1)
    chunk_bits = CHUNK.bit_length() - 1
    shift = jnp.right_shift(sel_row, chunk_bits) + 1
    select = (sel_col == jnp.bitwise_and(sel_row, CHUNK - 1) + CHUNK - shift).astype(BF16)
    tri = causal.astype(BF16)
    xc, xc_b, xm_b, pre, pre_t, bcum, bcum_t = [], [], [], [], [], [], []
    for b in seqs:
        xm_b.append(xm_ref[b])
        window = jnp.concatenate([prev_ref[b], xm_b[b]], axis=0)
        prev_ref[b] = xm_b[b]
        shifted = _dot(select, window)
        conv = xm_b[b].astype(F32) * cw_ref[B_CONV - 1:B_CONV, :] + cb_ref[...]
        for j in range(1, B_CONV):
            conv = conv + shifted[(j - 1) * CHUNK:j * CHUNK, :] * cw_ref[B_CONV - 1 - j:B_CONV - j, :]
        xc.append(_silu_tanh(conv))
        xc_b.append(xc[b].astype(BF16))

        pre.append(gate_ref[b] + gbias_ref[...])
        logf = jnp.minimum(pre[b], 0.0) - jnp.log(1.0 + jnp.exp(-jnp.abs(pre[b])))
        logf_hi = logf.astype(BF16)
        logf_lo = (logf - logf_hi.astype(F32)).astype(BF16)
        bcum.append(_dot(tri, logf_hi) + _dot(tri, logf_lo))
        pre_t.append(pre[b].T)
        bcum_t.append(bcum[b].T)

    units = [(b, h) for b in seqs for h in range(B_HEADS)]
    hcols = [slice(h * B_HEAD_DIM, (h + 1) * B_HEAD_DIM) for h in range(B_HEADS)]
    q = [_dot(xc_b[b][:, hcols[h]], wq_ref[h]) for b, h in units]
    k = [_dot(xc_b[b][:, hcols[h]], wk_ref[h]) * (B_HEAD_DIM ** -0.5) for b, h in units]
    v_b = [_dot(xm_b[b][:, hcols[h]], wv_ref[h]).astype(BF16) for b, h in units]
    idx = range(len(units))
    q_b = [q[i].astype(BF16) for i in idx]

    i_col = [pre[b][:, h:h + 1] for b, h in units]
    i_row = [pre_t[b][h:h + 1, :] for b, h in units]
    b_col = [bcum[b][:, B_HEADS + h:B_HEADS + h + 1] for b, h in units]
    b_row = [bcum_t[b][B_HEADS + h:B_HEADS + h + 1, :] for b, h in units]
    b_last = [b_col[i][CHUNK - 1:CHUNK, :] for i in idx]
    m_prev = [m_ref[b, h][:, 0:1] for b, h in units]
    c_prev = [c_ref[b, h] for b, h in units]
    n_prev = [n_ref[b, h] for b, h in units]

    a_col = [b_col[i] + m_prev[i] for i in idx]
    dmat = [jnp.where(causal, b_col[i] - b_row[i] + i_row[i], -jnp.inf) for i in idx]
    m_t = [jnp.maximum(a_col[i], jnp.max(dmat[i], axis=1, keepdims=True)) for i in idx]
    sc = [_dot_nt(q_b[i], k[i].astype(BF16)) * jnp.exp(dmat[i] - m_t[i]) for i in idx]
    w_inter = [jnp.exp(a_col[i] - m_t[i]) for i in idx]
    q_c = [_dot(q_b[i], c_prev[i].astype(BF16)) for i in idx]
    num = [_dot(sc[i].astype(BF16), v_b[i]) + w_inter[i] * q_c[i] for i in idx]
    den = [jnp.sum(sc[i], axis=1, keepdims=True)
           + w_inter[i] * jnp.sum(q[i] * n_prev[i], axis=1, keepdims=True) for i in idx]
    h_t = [num[i] / jnp.maximum(jnp.abs(den[i]), jnp.exp(-m_t[i])) for i in idx]

    g_col = [b_last[i] - b_col[i] + i_col[i] for i in idx]
    g_row = [b_last[i] - b_row[i] + i_row[i] for i in idx]
    m_new = [jnp.maximum(b_last[i] + m_prev[i], jnp.max(g_row[i], axis=1, keepdims=True)) for i in idx]
    kw = [k[i] * jnp.exp(g_col[i] - m_new[i]) for i in idx]
    decay = [jnp.exp(b_last[i] + m_prev[i] - m_new[i]) for i in idx]
    for i, (b, h) in enumerate(units):
        c_ref[b, h] = decay[i] * c_prev[i] + _dot_tn(kw[i].astype(BF16), v_b[i])
    for i, (b, h) in enumerate(units):
        n_ref[b, h] = decay[i] * n_prev[i] + jnp.sum(kw[i], axis=0, keepdims=True)
        m_ref[b, h] = jnp.broadcast_to(m_new[i], (1, LANES))

    for i, (b, h) in enumerate(units):
        og = og_ref[b, :, hcols[h]].astype(F32)
        zb = zb_ref[b, :, hcols[h]].astype(F32)
        gated = h_t[i] * (0.5 + 0.5 * jnp.tanh(0.5 * og))
        h_n = _layernorm_rows(gated, gn_g_ref[:, hcols[h]])
        y_b = (h_n + skip_ref[:, hcols[h]] * xc[b][:, hcols[h]]) * _silu_tanh(zb)
        o_ref[b, :, D_MODEL + h * B_HEAD_DIM:D_MODEL + (h + 1) * B_HEAD_DIM] = y_b.astype(BF16)


def _even_mixer(proj, gates, ln_g, ws, bs_t, cw, cb, wq, wk, wv, gbias, gn_g, skip, batch, seq):
    t = proj.shape[0]
    nc = seq // CHUNK
    mb = MIX_BATCH if batch % MIX_BATCH == 0 else 1
    proj3 = proj.reshape(batch, seq, EVEN_MAIN)
    gates3 = gates.reshape(batch, seq, LANES)

    def col_block(j):
        return pl.BlockSpec((mb, CHUNK, D_MODEL), lambda b, c, j=j: (b, c, j))

    def whole(a):
        nd = a.ndim
        return pl.BlockSpec(a.shape, lambda b, c, nd=nd: (0,) * nd)

    consts = (ln_g, ws, bs_t, cw, cb, wq, wk, wv, gbias, gn_g, skip)
    out = pl.pallas_call(
        _even_mixer_kernel,
        grid=(batch // mb, nc),
        in_specs=[col_block(j) for j in range(6)]
        + [pl.BlockSpec((mb, CHUNK, LANES), lambda b, c: (b, c, 0))]
        + [whole(a) for a in consts],
        out_specs=pl.BlockSpec((mb, CHUNK, 2 * D_MODEL), lambda b, c: (b, c, 0)),
        out_shape=jax.ShapeDtypeStruct((batch, seq, 2 * D_MODEL), BF16),
        scratch_shapes=[
            pltpu.VMEM((mb, CHUNK, D_MODEL), BF16),
            pltpu.VMEM((mb, B_HEADS, B_HEAD_DIM, B_HEAD_DIM), F32),
            pltpu.VMEM((mb, B_HEADS, 1, B_HEAD_DIM), F32),
            pltpu.VMEM((mb, B_HEADS, 1, LANES), F32),
        ],
        compiler_params=_params(("parallel", "arbitrary")),
        name="even_mixer",
    )(proj3, proj3, proj3, proj3, proj3, proj3, gates3, *consts)
    return out.reshape(t, 2 * D_MODEL)


ATTN_TILE = 512
ATTN_ROWS = 64
ATTN_BAND = 256
ATTN_OFF_ROWS = 256
ATTN_UNROLL = 8


def _attn_kernel(lamp_ref, subln_ref, q_ref, k_ref, v_ref, z_ref, o_ref,
                 vext_ref, qc_ref, p0_ref, p1_ref, alpha0_ref, alpha1_ref, m_ref, acc_ref,
                 *, tile, lam_init):
    p_refs, alpha_refs = (p0_ref, p1_ref), (alpha0_ref, alpha1_ref)
    seq = q_ref.shape[0]
    nq = seq // tile
    rows = min(ATTN_ROWS, tile)
    half = min(ATTN_BAND, tile)
    off_rows = min(ATTN_OFF_ROWS, tile)

    lane = lax.broadcasted_iota(jnp.int32, (seq, LANES), 1)
    zero = jnp.zeros((), BF16)

    vext_ref[:, :LANES] = v_ref[...]
    vext_ref[:, LANES:] = jnp.ones((seq, LANES), BF16)
    q_all = q_ref[...]
    qc_ref[0] = jnp.where(lane < C_HEAD_DIM, q_all, zero)
    qc_ref[1] = jnp.where(lane >= C_HEAD_DIM, q_all, zero)

    def start(i):
        return pl.multiple_of(jnp.asarray(i, jnp.int32) * tile, tile)

    bands = [(r0, r0 + half) for r0 in range(0, tile, half)]

    diag_units = [(c, r0, half, keys) for c in range(2) for r0, keys in bands]
    off_units = [(c, r0, off_rows, tile) for c in range(2) for r0 in range(0, tile, off_rows)]

    def softmax_unit(qi, j, slot, unit, diagonal):
        c, r0, nrows, keys = unit
        k = k_ref[pl.ds(start(j), keys), :]
        s_all = _dot_nt(qc_ref[c, pl.ds(start(qi) + r0, nrows), :], k)
        for r in range(0, nrows, rows):
            dst = slice(r0 + r, r0 + r + rows)
            s = s_all[r:r + rows, :]
            if diagonal:
                keep = (lax.broadcasted_iota(jnp.int32, (rows, keys), 1)
                        <= lax.broadcasted_iota(jnp.int32, (rows, keys), 0) + (r0 + r))
                s = jnp.where(keep, s, -jnp.inf)
                m_new = jnp.broadcast_to(jnp.max(s, axis=1, keepdims=True), (rows, LANES))
            else:
                m_prev = m_ref[qi, c, dst, :]
                m_new = jnp.maximum(m_prev, jnp.max(s, axis=1, keepdims=True))
                alpha_refs[slot][c, dst, :] = jnp.exp2(m_prev - m_new)
            p = jnp.exp2(s - jnp.concatenate([m_new] * (keys // LANES), axis=1))
            p_refs[slot][c, dst, :keys] = p.astype(BF16)
            m_ref[qi, c, dst, :] = m_new

    def values_unit(qi, j, slot, unit, diagonal):
        c, r0, nrows, keys = unit
        dst = slice(r0, r0 + nrows)
        pv = _dot(p_refs[slot][c, dst, :keys], vext_ref[pl.ds(start(j), keys), :])
        if diagonal:
            acc_ref[qi, c, dst, :] = pv
        else:
            alpha = alpha_refs[slot][c, dst, :]
            acc_ref[qi, c, dst, :] = jnp.concatenate([alpha, alpha], axis=1) * acc_ref[qi, c, dst, :] + pv

    def pipeline(blocks, diagonal, unroll):
        nblk = len(blocks)
        units = diag_units if diagonal else off_units

        def step(cur, prev, t):
            for group in ([units] if diagonal else [[unit] for unit in units]):
                for unit in group:
                    if cur is not None:
                        softmax_unit(*cur, t % 2, unit, diagonal)
                for unit in group:
                    if prev is not None:
                        values_unit(*prev, (t - 1) % 2, unit, diagonal)

        def static_step(t):
            step(blocks[t] if t < nblk else None, blocks[t - 1] if 1 <= t <= nblk else None, t)

        def next_block(qi, j):
            if diagonal:
                return qi + 1, j + 1
            wrap = j + 1 == qi
            return jnp.where(wrap, qi + 1, qi), jnp.where(wrap, 0, j + 1)

        static_step(0)
        n_trips = max(nblk - 1, 0) // unroll
        if n_trips > 0:
            def body(i, carry):
                prev, cur = carry
                for u in range(unroll):
                    step(cur, prev, 1 + u)
                    prev, cur = cur, next_block(*cur)
                return prev, cur

            as_i32 = lambda b: (jnp.int32(b[0]), jnp.int32(b[1]))
            lax.fori_loop(0, n_trips, body, (as_i32(blocks[0]), as_i32(blocks[1])))
        for t in range(1 + unroll * n_trips, nblk + 1):
            static_step(t)

    pipeline([(qi, qi) for qi in range(nq)], True, 2)
    if nq > 1:
        pipeline([(qi, j) for qi in range(1, nq) for j in range(qi)], False, ATTN_UNROLL)

    lp = lamp_ref[...]
    lam = (jnp.exp(jnp.sum(lp[0:1] * lp[1:2], axis=-1, keepdims=True))
           - jnp.exp(jnp.sum(lp[2:3] * lp[3:4], axis=-1, keepdims=True)) + lam_init)

    def finalize(qi, carry):
        a0, a1 = acc_ref[qi, 0], acc_ref[qi, 1]
        o = a0[:, :LANES] / a0[:, LANES:] - lam * (a1[:, :LANES] / a1[:, LANES:])
        o = _rmsnorm_rows(o, subln_ref[...]) * (1.0 - lam_init)
        z = z_ref[pl.ds(start(qi), tile), :].astype(F32)
        o_ref[pl.ds(start(qi), tile), :] = (o * _silu(z)).astype(BF16)
        return carry

    lax.fori_loop(0, nq, finalize, 0)


def _attention(proj, lamp, subln, batch, seq, lam_init):
    t = proj.shape[0]
    tile = min(ATTN_TILE, seq)
    nq = seq // tile

    def head_block(part):
        return pl.BlockSpec((seq, LANES), lambda b, h, part=part: (b, part * C_HEADS + h))

    slots = lambda shape, dtype: [pltpu.VMEM(shape, dtype), pltpu.VMEM(shape, dtype)]
    return pl.pallas_call(
        functools.partial(_attn_kernel, tile=tile, lam_init=lam_init),
        grid=(batch, C_HEADS),
        in_specs=[
            pl.BlockSpec(lamp.shape, lambda b, h: (0, 0)),
            pl.BlockSpec(subln.shape, lambda b, h: (0, 0)),
            head_block(0), head_block(1), head_block(2), head_block(3),
        ],
        out_specs=pl.BlockSpec((seq, LANES), lambda b, h: (b, h)),
        out_shape=jax.ShapeDtypeStruct((t, C_HEADS * LANES), BF16),
        scratch_shapes=[pltpu.VMEM((seq, 2 * LANES), BF16),
                        pltpu.VMEM((2, seq, LANES), BF16)]
        + slots((2, tile, tile), BF16)
        + slots((2, tile, LANES), F32)
        + [pltpu.VMEM((nq, 2, tile, LANES), F32),
           pltpu.VMEM((nq, 2, tile, 2 * LANES), F32)],
        compiler_params=_params(("parallel", "parallel")),
        name="diff_attention",
    )(lamp, subln, proj, proj, proj, proj)


def _lambda_init(layer):
    return 0.8 - 0.6 * math.exp(-0.3 * layer)


def _rope_tables(seq):
    half = C_HEAD_DIM // 2
    inv = ROPE_THETA ** (-jnp.arange(0, C_HEAD_DIM, 2, dtype=F32) / C_HEAD_DIM)
    ang = jnp.arange(seq, dtype=F32)[:, None] * inv[None, :]
    cos = jnp.tile(jnp.cos(ang), (1, LANES // half))
    sin = jnp.tile(jnp.sin(ang), (1, LANES // half))
    first_half = (jnp.arange(LANES) % C_HEAD_DIM) < half
    return cos, jnp.where(first_half, -sin, 0.0), jnp.where(first_half, 0.0, sin)


def kernel(x, norm_g, ev_w_in, ev_w_out, a_ln_g, a_ws, a_bs, b_conv_w, b_conv_b, b_wq, b_wk, b_wv,
           b_ig_b, b_fg_b, b_gn_g, b_skip, od_w_in, od_w_out, c_lam_q1, c_lam_k1, c_lam_q2,
           c_lam_k2, c_subln_g, final_g):
    batch, seq, d = x.shape
    depth = norm_g.shape[0]
    assert d == D_MODEL and seq % PROJ_TM == 0 and (batch * seq) % OUT_TM == 0
    x2 = x.reshape(batch * seq, d)
    cos, sin_lo, sin_hi = _rope_tables(seq)
    fg = final_g.reshape(1, d)
    row = lambda a: a.reshape(1, -1)
    for layer in range(depth):
        g = row(norm_g[layer])
        final = layer == depth - 1
        if layer % 2 == 0:
            e = layer // 2
            w_in = ev_w_in[e]
            w_main = jnp.concatenate([w_in[:, :GATE_LO], w_in[:, GATE_LO + 2 * B_HEADS:]], axis=1).astype(BF16)
            w_gate = jnp.pad(w_in[:, GATE_LO:GATE_LO + 2 * B_HEADS], ((0, 0), (0, LANES - 2 * B_HEADS))).astype(BF16)
            proj, gates = _even_proj(x2, g, w_main, w_gate)
            gbias = jnp.pad(jnp.concatenate([b_ig_b[e], b_fg_b[e]]), (0, LANES - 2 * B_HEADS)).reshape(1, LANES)
            y = _even_mixer(proj, gates, row(a_ln_g[e]), a_ws[e].astype(BF16), a_bs[e].T, b_conv_w[e],
                            row(b_conv_b[e]), b_wq[e].astype(BF16), b_wk[e].astype(BF16),
                            b_wv[e].astype(BF16), gbias, row(b_gn_g[e]), row(b_skip[e]), batch, seq)
            x2 = _out_proj(y, ev_w_out[e].astype(BF16), x2, fg, final)
        else:
            o = layer // 2
            proj = _odd_proj(x2, g, od_w_in[o].astype(BF16), cos, sin_lo, sin_hi, seq)
            lamp = jnp.stack([c_lam_q1[o], c_lam_k1[o], c_lam_q2[o], c_lam_k2[o]])
            y = _attention(proj, lamp, row(c_subln_g[o]), batch, seq, _lambda_init(layer))
            x2 = _out_proj(y, od_w_out[o].astype(BF16), x2, fg, final)
    return x2.reshape(batch, seq, d)
```

```python
import functools
import math

import jax
import jax.numpy as jnp
from jax import lax
from jax.experimental import pallas as pl
from jax.experimental.pallas import tpu as pltpu

F32 = jnp.float32
BF16 = jnp.bfloat16

D_MODEL = 1024
A_GROUPS = 8
CHUNK = 128
B_HEADS = 4
B_HEAD_DIM = 256
B_CONV = 4
C_HEADS = 16
C_HEAD_DIM = 64
LANES = 128
ROPE_THETA = 10000.0
NORM_EPS = 1e-6
LOG2E = 1.4426950408889634
EVEN_MAIN = 6 * D_MODEL
ODD_IN = 8 * D_MODEL
GATE_LO = 5 * D_MODEL

VMEM_LIMIT_BYTES = 56 * 1024 * 1024


def _params(semantics):
    return pltpu.CompilerParams(dimension_semantics=semantics, vmem_limit_bytes=VMEM_LIMIT_BYTES)


def _rmsnorm_rows(x, g):
    return x * lax.rsqrt(jnp.mean(x * x, axis=-1, keepdims=True) + NORM_EPS) * g


def _layernorm_rows(x, g):
    xc = x - jnp.mean(x, axis=-1, keepdims=True)
    return xc * lax.rsqrt(jnp.mean(xc * xc, axis=-1, keepdims=True) + NORM_EPS) * g


def _dot(a, b):
    return jnp.dot(a, b, preferred_element_type=F32)


def _dot_nt(a, b):
    return lax.dot_general(a, b, (((1,), (1,)), ((), ())), preferred_element_type=F32)


def _dot_tn(a, b):
    return lax.dot_general(a, b, (((0,), (0,)), ((), ())), preferred_element_type=F32)


PROJ_TM = 512
PROJ_TN = 512


def _even_proj_kernel(x_ref, g_ref, w_ref, wg_ref, o_ref, gate_ref):
    hn = _rmsnorm_rows(x_ref[...], g_ref[...]).astype(BF16)
    gate_ref[...] = _dot(hn, wg_ref[...])
    for c in range(EVEN_MAIN // PROJ_TN):
        cols = slice(c * PROJ_TN, (c + 1) * PROJ_TN)
        o_ref[:, cols] = _dot(hn, w_ref[:, cols]).astype(BF16)


def _odd_proj_kernel(x_ref, g_ref, w_ref, cos_ref, sin_lo_ref, sin_hi_ref, o_ref):
    hn = _rmsnorm_rows(x_ref[...], g_ref[...]).astype(BF16)
    cos, sin_lo, sin_hi = cos_ref[...], sin_lo_ref[...], sin_hi_ref[...]
    q_cols = C_HEADS * 2 * C_HEAD_DIM
    for c in range(ODD_IN // PROJ_TN):
        cols = slice(c * PROJ_TN, (c + 1) * PROJ_TN)
        acc = _dot(hn, w_ref[:, cols])
        if c * PROJ_TN < 2 * q_cols:
            scale = C_HEAD_DIM ** -0.5 * LOG2E if c * PROJ_TN < q_cols else 1.0
            for j in range(PROJ_TN // LANES):
                blk = acc[:, j * LANES:(j + 1) * LANES]
                rot = (blk * cos + pltpu.roll(blk, LANES - C_HEAD_DIM // 2, 1) * sin_lo
                       + pltpu.roll(blk, C_HEAD_DIM // 2, 1) * sin_hi)
                o_ref[:, c * PROJ_TN + j * LANES:c * PROJ_TN + (j + 1) * LANES] = (rot * scale).astype(BF16)
        else:
            o_ref[:, cols] = acc.astype(BF16)


def _even_proj(x2, g, w_main, w_gate):
    t = x2.shape[0]
    return pl.pallas_call(
        _even_proj_kernel,
        grid=(t // PROJ_TM,),
        in_specs=[
            pl.BlockSpec((PROJ_TM, D_MODEL), lambda i: (i, 0)),
            pl.BlockSpec((1, D_MODEL), lambda i: (0, 0)),
            pl.BlockSpec((D_MODEL, EVEN_MAIN), lambda i: (0, 0), pipeline_mode=pl.Buffered(1)),
            pl.BlockSpec((D_MODEL, LANES), lambda i: (0, 0)),
        ],
        out_specs=[
            pl.BlockSpec((PROJ_TM, EVEN_MAIN), lambda i: (i, 0)),
            pl.BlockSpec((PROJ_TM, LANES), lambda i: (i, 0)),
        ],
        out_shape=[
            jax.ShapeDtypeStruct((t, EVEN_MAIN), BF16),
            jax.ShapeDtypeStruct((t, LANES), F32),
        ],
        compiler_params=_params(("parallel",)),
        name="even_proj",
    )(x2, g, w_main, w_gate)


def _odd_proj(x2, g, w, cos, sin_lo, sin_hi, seq):
    t = x2.shape[0]
    pos_blocks = seq // PROJ_TM
    tab = pl.BlockSpec((PROJ_TM, LANES), lambda i: (i % pos_blocks, 0))
    return pl.pallas_call(
        _odd_proj_kernel,
        grid=(t // PROJ_TM,),
        in_specs=[
            pl.BlockSpec((PROJ_TM, D_MODEL), lambda i: (i, 0)),
            pl.BlockSpec((1, D_MODEL), lambda i: (0, 0)),
            pl.BlockSpec((D_MODEL, ODD_IN), lambda i: (0, 0), pipeline_mode=pl.Buffered(1)),
            tab, tab, tab,
        ],
        out_specs=pl.BlockSpec((PROJ_TM, ODD_IN), lambda i: (i, 0)),
        out_shape=jax.ShapeDtypeStruct((t, ODD_IN), BF16),
        compiler_params=_params(("parallel",)),
        name="odd_proj",
    )(x2, g, w, cos, sin_lo, sin_hi)


OUT_TM = 512


def _out_proj_kernel(y_ref, w_ref, x_ref, fg_ref, o_ref, *, final):
    acc = _dot(y_ref[...], w_ref[...]) + x_ref[...]
    if final:
        acc = _rmsnorm_rows(acc, fg_ref[...])
    o_ref[...] = acc


def _out_proj(y, w, x2, final_g, final):
    t, k = y.shape
    return pl.pallas_call(
        functools.partial(_out_proj_kernel, final=final),
        grid=(t // OUT_TM,),
        in_specs=[
            pl.BlockSpec((OUT_TM, k), lambda i: (i, 0)),
            pl.BlockSpec((k, D_MODEL), lambda i: (0, 0)),
            pl.BlockSpec((OUT_TM, D_MODEL), lambda i: (i, 0)),
            pl.BlockSpec((1, D_MODEL), lambda i: (0, 0)),
        ],
        out_specs=pl.BlockSpec((OUT_TM, D_MODEL), lambda i: (i, 0)),
        out_shape=jax.ShapeDtypeStruct((t, D_MODEL), F32),
        compiler_params=_params(("parallel",)),
        name="out_proj_final" if final else "out_proj",
    )(y, w, x2, final_g)


MIX_BATCH = 4
GELU_C0 = math.sqrt(2.0 / math.pi)
GELU_C1 = 0.044715 * GELU_C0


def _gelu_tanh(x):
    hx = 0.5 * x
    return hx + hx * jnp.tanh(x * (GELU_C0 + GELU_C1 * (x * x)))


def _silu_tanh(x):
    hx = 0.5 * x
    return hx + hx * jnp.tanh(hx)


def _even_mixer_kernel(u_ref, va_ref, za_ref, xm_ref, og_ref, zb_ref, gate_ref,
                       ln_g_ref, ws_ref, bs_ref, cw_ref, cb_ref, wq_ref, wk_ref, wv_ref,
                       gbias_ref, gn_g_ref, skip_ref,
                       o_ref, prev_ref, c_ref, n_ref, m_ref):
    @pl.when(pl.program_id(1) == 0)
    def _():
        prev_ref[...] = jnp.zeros_like(prev_ref)
        c_ref[...] = jnp.zeros_like(c_ref)
        n_ref[...] = jnp.zeros_like(n_ref)
        m_ref[...] = jnp.zeros_like(m_ref)

    seqs = range(u_ref.shape[0])
    row = lax.broadcasted_iota(jnp.int32, (CHUNK, CHUNK), 0)
    col = lax.broadcasted_iota(jnp.int32, (CHUNK, CHUNK), 1)
    causal = col <= row

    w_s = [jnp.where(causal, ws_ref[g], jnp.zeros((), BF16)) for g in range(A_GROUPS)]
    for b in seqs:
        u = _gelu_tanh(u_ref[b].astype(F32))
        va = _layernorm_rows(_gelu_tanh(va_ref[b].astype(F32)), ln_g_ref[...]).astype(BF16)
        za = za_ref[b].astype(F32)
        for g in range(A_GROUPS):
            cols = slice(g * CHUNK, (g + 1) * CHUNK)
            sg = _dot(w_s[g], va[:, cols]) + bs_ref[:, g:g + 1]
            o_ref[b, :, cols] = (u[:, cols] * sg * _silu_tanh(za[:, cols])).astype(BF16)

    sel_row = lax.broadcasted_iota(jnp.int32, ((B_CONV - 1) * CHUNK, 2 * CHUNK), 0)
    sel_col = lax.broadcasted_iota(jnp.int32, ((B_CONV - 1) * CHUNK, 2 * CHUNK), 1)
    chunk_bits = CHUNK.bit_length() - 1
    shift = jnp.right_shift(sel_row, chunk_bits) + 1
    select = (sel_col == jnp.bitwise_and(sel_row, CHUNK - 1) + CHUNK - shift).astype(BF16)
    tri = causal.astype(BF16)
    xc, xc_b, xm_b, pre, pre_t, bcum, bcum_t = [], [], [], [], [], [], []
    for b in seqs:
        xm_b.append(xm_ref[b])
        window = jnp.concatenate([prev_ref[b], xm_b[b]], axis=0)
        prev_ref[b] = xm_b[b]
        shifted = _dot(select, window)
        conv = xm_b[b].astype(F32) * cw_ref[B_CONV - 1:B_CONV, :] + cb_ref[...]
        for j in range(1, B_CONV):
            conv = conv + shifted[(j - 1) * CHUNK:j * CHUNK, :] * cw_ref[B_CONV - 1 - j:B_CONV - j, :]
        xc.append(_silu_tanh(conv))
        xc_b.append(xc[b].astype(BF16))

        pre.append(gate_ref[b] + gbias_ref[...])
        logf = jnp.minimum(pre[b], 0.0) - jnp.log(1.0 + jnp.exp(-jnp.abs(pre[b])))
        logf_hi = logf.astype(BF16)
        logf_lo = (logf - logf_hi.astype(F32)).astype(BF16)
        bcum.append(_dot(tri, logf_hi) + _dot(tri, logf_lo))
        pre_t.append(pre[b].T)
        bcum_t.append(bcum[b].T)

    units = [(b, h) for b in seqs for h in range(B_HEADS)]
    hcols = [slice(h * B_HEAD_DIM, (h + 1) * B_HEAD_DIM) for h in range(B_HEADS)]
    q = [_dot(xc_b[b][:, hcols[h]], wq_ref[h]) for b, h in units]
    k = [_dot(xc_b[b][:, hcols[h]], wk_ref[h]) * (B_HEAD_DIM ** -0.5) for b, h in units]
    v_b = [_dot(xm_b[b][:, hcols[h]], wv_ref[h]).astype(BF16) for b, h in units]
    idx = range(len(units))
    q_b = [q[i].astype(BF16) for i in idx]

    i_col = [pre[b][:, h:h + 1] for b, h in units]
    i_row = [pre_t[b][h:h + 1, :] for b, h in units]
    b_col = [bcum[b][:, B_HEADS + h:B_HEADS + h + 1] for b, h in units]
    b_row = [bcum_t[b][B_HEADS + h:B_HEADS + h + 1, :] for b, h in units]
    b_last = [b_col[i][CHUNK - 1:CHUNK, :] for i in idx]
    m_prev = [m_ref[b, h][:, 0:1] for b, h in units]
    c_prev = [c_ref[b, h] for b, h in units]
    n_prev = [n_ref[b, h] for b, h in units]

    a_col = [b_col[i] + m_prev[i] for i in idx]
    dmat = [jnp.where(causal, b_col[i] - b_row[i] + i_row[i], -jnp.inf) for i in idx]
    m_t = [jnp.maximum(a_col[i], jnp.max(dmat[i], axis=1, keepdims=True)) for i in idx]
    sc = [_dot_nt(q_b[i], k[i].astype(BF16)) * jnp.exp(dmat[i] - m_t[i]) for i in idx]
    w_inter = [jnp.exp(a_col[i] - m_t[i]) for i in idx]
    q_c = [_dot(q_b[i], c_prev[i].astype(BF16)) for i in idx]
    num = [_dot(sc[i].astype(BF16), v_b[i]) + w_inter[i] * q_c[i] for i in idx]
    den = [jnp.sum(sc[i], axis=1, keepdims=True)
           + w_inter[i] * jnp.sum(q[i] * n_prev[i], axis=1, keepdims=True) for i in idx]
    h_t = [num[i] / jnp.maximum(jnp.abs(den[i]), jnp.exp(-m_t[i])) for i in idx]

    g_col = [b_last[i] - b_col[i] + i_col[i] for i in idx]
    g_row = [b_last[i] - b_row[i] + i_row[i] for i in idx]
    m_new = [jnp.maximum(b_last[i] + m_prev[i], jnp.max(g_row[i], axis=1, keepdims=True)) for i in idx]
    kw = [k[i] * jnp.exp(g_col[i] - m_new[i]) for i in idx]
    decay = [jnp.exp(b_last[i] + m_prev[i] - m_new[i]) for i in idx]
    for i, (b, h) in enumerate(units):
        c_ref[b, h] = decay[i] * c_prev[i] + _dot_tn(kw[i].astype(BF16), v_b[i])
    for i, (b, h) in enumerate(units):
        n_ref[b, h] = decay[i] * n_prev[i] + jnp.sum(kw[i], axis=0, keepdims=True)
        m_ref[b, h] = jnp.broadcast_to(m_new[i], (1, LANES))

    for i, (b, h) in enumerate(units):
        og = og_ref[b, :, hcols[h]].astype(F32)
        zb = zb_ref[b, :, hcols[h]].astype(F32)
        gated = h_t[i] * (0.5 + 0.5 * jnp.tanh(0.5 * og))
        h_n = _layernorm_rows(gated, gn_g_ref[:, hcols[h]])
        y_b = (h_n + skip_ref[:, hcols[h]] * xc[b][:, hcols[h]]) * _silu_tanh(zb)
        o_ref[b, :, D_MODEL + h * B_HEAD_DIM:D_MODEL + (h + 1) * B_HEAD_DIM] = y_b.astype(BF16)


def _even_mixer(proj, gates, ln_g, ws, bs_t, cw, cb, wq, wk, wv, gbias, gn_g, skip, batch, seq):
    t = proj.shape[0]
    nc = seq // CHUNK
    mb = MIX_BATCH if batch % MIX_BATCH == 0 else 1
    proj3 = proj.reshape(batch, seq, EVEN_MAIN)
    gates3 = gates.reshape(batch, seq, LANES)

    def col_block(j):
        return pl.BlockSpec((mb, CHUNK, D_MODEL), lambda b, c, j=j: (b, c, j))

    def whole(a):
        nd = a.ndim
        return pl.BlockSpec(a.shape, lambda b, c, nd=nd: (0,) * nd)

    consts = (ln_g, ws, bs_t, cw, cb, wq, wk, wv, gbias, gn_g, skip)
    out = pl.pallas_call(
        _even_mixer_kernel,
        grid=(batch // mb, nc),
        in_specs=[col_block(j) for j in range(6)]
        + [pl.BlockSpec((mb, CHUNK, LANES), lambda b, c: (b, c, 0))]
        + [whole(a) for a in consts],
        out_specs=pl.BlockSpec((mb, CHUNK, 2 * D_MODEL), lambda b, c: (b, c, 0)),
        out_shape=jax.ShapeDtypeStruct((batch, seq, 2 * D_MODEL), BF16),
        scratch_shapes=[
            pltpu.VMEM((mb, CHUNK, D_MODEL), BF16),
            pltpu.VMEM((mb, B_HEADS, B_HEAD_DIM, B_HEAD_DIM), F32),
            pltpu.VMEM((mb, B_HEADS, 1, B_HEAD_DIM), F32),
            pltpu.VMEM((mb, B_HEADS, 1, LANES), F32),
        ],
        compiler_params=_params(("parallel", "arbitrary")),
        name="even_mixer",
    )(proj3, proj3, proj3, proj3, proj3, proj3, gates3, *consts)
    return out.reshape(t, 2 * D_MODEL)


ATTN_TILE = 512
ATTN_ROWS = 64
ATTN_BAND = 256
ATTN_OFF_ROWS = 256
ATTN_UNROLL = 12


def _attn_kernel(lamp_ref, subln_ref, q_ref, k_ref, v_ref, z_ref, o_ref,
                 vext_ref, qc_ref, p0_ref, p1_ref, alpha0_ref, alpha1_ref, m_ref, acc_ref,
                 *, tile, lam_init):
    p_refs, alpha_refs = (p0_ref, p1_ref), (alpha0_ref, alpha1_ref)
    seq = q_ref.shape[0]
    nq = seq // tile
    rows = min(ATTN_ROWS, tile)
    half = min(ATTN_BAND, tile)
    off_rows = min(ATTN_OFF_ROWS, tile)

    lane = lax.broadcasted_iota(jnp.int32, (tile, LANES), 1)
    zero = jnp.zeros((), BF16)

    def start(i):
        return pl.multiple_of(jnp.asarray(i, jnp.int32) * tile, tile)

    def prepare_tile(i):
        at = pl.ds(start(i), tile)
        vext_ref[at, :LANES] = v_ref[at, :]
        vext_ref[at, LANES:] = jnp.ones((tile, LANES), BF16)
        q = q_ref[at, :]
        qc_ref[0, at, :] = jnp.where(lane < C_HEAD_DIM, q, zero)
        qc_ref[1, at, :] = jnp.where(lane >= C_HEAD_DIM, q, zero)

    bands = [(r0, r0 + half) for r0 in range(0, tile, half)]

    diag_units = [(c, r0, half, keys) for c in range(2) for r0, keys in bands]
    off_units = [(c, r0, off_rows, tile) for c in range(2) for r0 in range(0, tile, off_rows)]

    def softmax_unit(qi, j, slot, unit, diagonal):
        c, r0, nrows, keys = unit
        k = k_ref[pl.ds(start(j), keys), :]
        s_all = _dot_nt(qc_ref[c, pl.ds(start(qi) + r0, nrows), :], k)
        for r in range(0, nrows, rows):
            dst = slice(r0 + r, r0 + r + rows)
            s = s_all[r:r + rows, :]
            if diagonal:
                keep = (lax.broadcasted_iota(jnp.int32, (rows, keys), 1)
                        <= lax.broadcasted_iota(jnp.int32, (rows, keys), 0) + (r0 + r))
                s = jnp.where(keep, s, -jnp.inf)
                m_new = jnp.broadcast_to(jnp.max(s, axis=1, keepdims=True), (rows, LANES))
            else:
                m_prev = m_ref[qi, c, dst, :]
                m_new = jnp.maximum(m_prev, jnp.max(s, axis=1, keepdims=True))
                alpha_refs[slot][c, dst, :] = jnp.exp2(m_prev - m_new)
            p = jnp.exp2(s - jnp.concatenate([m_new] * (keys // LANES), axis=1))
            p_refs[slot][c, dst, :keys] = p.astype(BF16)
            m_ref[qi, c, dst, :] = m_new

    def values_unit(qi, j, slot, unit, diagonal):
        c, r0, nrows, keys = unit
        dst = slice(r0, r0 + nrows)
        pv = _dot(p_refs[slot][c, dst, :keys], vext_ref[pl.ds(start(j), keys), :])
        if diagonal:
            acc_ref[qi, c, dst, :] = pv
        else:
            alpha = alpha_refs[slot][c, dst, :]
            acc_ref[qi, c, dst, :] = jnp.concatenate([alpha, alpha], axis=1) * acc_ref[qi, c, dst, :] + pv

    def pipeline(blocks, diagonal, unroll):
        nblk = len(blocks)
        units = diag_units if diagonal else off_units

        def step(cur, prev, t, prepare=None):
            for group in ([units] if diagonal else [[unit] for unit in units]):
                for unit in group:
                    if cur is not None:
                        softmax_unit(*cur, t % 2, unit, diagonal)
                for unit in group:
                    if prev is not None:
                        values_unit(*prev, (t - 1) % 2, unit, diagonal)
            if prepare is not None:
                prepare_tile(prepare)

        def static_step(t):
            step(blocks[t] if t < nblk else None, blocks[t - 1] if 1 <= t <= nblk else None, t,
                 t + 1 if diagonal and t + 1 < nblk else None)

        def next_block(qi, j):
            if diagonal:
                return qi + 1, j + 1
            wrap = j + 1 == qi
            return jnp.where(wrap, qi + 1, qi), jnp.where(wrap, 0, j + 1)

        if diagonal:
            prepare_tile(0)
        static_step(0)
        n_trips = max(nblk - (2 if diagonal else 1), 0) // unroll
        if n_trips > 0:
            def body(i, carry):
                prev, cur = carry
                for u in range(unroll):
                    step(cur, prev, 1 + u, cur[0] + 1 if diagonal else None)
                    prev, cur = cur, next_block(*cur)
                return prev, cur

            as_i32 = lambda b: (jnp.int32(b[0]), jnp.int32(b[1]))
            lax.fori_loop(0, n_trips, body, (as_i32(blocks[0]), as_i32(blocks[1])))
        for t in range(1 + unroll * n_trips, nblk + 1):
            static_step(t)

    lp = lamp_ref[...]
    lam = (jnp.exp(jnp.sum(lp[0:1] * lp[1:2], axis=-1, keepdims=True))
           - jnp.exp(jnp.sum(lp[2:3] * lp[3:4], axis=-1, keepdims=True)) + lam_init)

    subln_gain = subln_ref[...] * (1.0 - lam_init)

    def finalize(qi, carry):
        a0, a1 = acc_ref[qi, 0], acc_ref[qi, 1]
        o = a0[:, :LANES] / a0[:, LANES:] - lam * (a1[:, :LANES] / a1[:, LANES:])
        z = z_ref[pl.ds(start(qi), tile), :].astype(F32)
        o_ref[pl.ds(start(qi), tile), :] = (_rmsnorm_rows(o, subln_gain) * _silu_tanh(z)).astype(BF16)
        return carry

    pipeline([(qi, qi) for qi in range(nq)], True, 2)
    if nq > 1:
        pipeline([(qi, j) for qi in range(1, nq) for j in range(qi)], False, ATTN_UNROLL)
    lax.fori_loop(0, nq, finalize, 0)


def _attention(proj, lamp, subln, batch, seq, lam_init):
    t = proj.shape[0]
    tile = min(ATTN_TILE, seq)
    nq = seq // tile

    def head_block(part):
        return pl.BlockSpec((seq, LANES), lambda b, h, part=part: (b, part * C_HEADS + h))

    slots = lambda shape, dtype: [pltpu.VMEM(shape, dtype), pltpu.VMEM(shape, dtype)]
    return pl.pallas_call(
        functools.partial(_attn_kernel, tile=tile, lam_init=lam_init),
        grid=(batch, C_HEADS),
        in_specs=[
            pl.BlockSpec(lamp.shape, lambda b, h: (0, 0)),
            pl.BlockSpec(subln.shape, lambda b, h: (0, 0)),
            head_block(0), head_block(1), head_block(2), head_block(3),
        ],
        out_specs=pl.BlockSpec((seq, LANES), lambda b, h: (b, h)),
        out_shape=jax.ShapeDtypeStruct((t, C_HEADS * LANES), BF16),
        scratch_shapes=[pltpu.VMEM((seq, 2 * LANES), BF16),
                        pltpu.VMEM((2, seq, LANES), BF16)]
        + slots((2, tile, tile), BF16)
        + slots((2, tile, LANES), F32)
        + [pltpu.VMEM((nq, 2, tile, LANES), F32),
           pltpu.VMEM((nq, 2, tile, 2 * LANES), F32)],
        compiler_params=_params(("parallel", "parallel")),
        name="diff_attention",
    )(lamp, subln, proj, proj, proj, proj)


def _lambda_init(layer):
    return 0.8 - 0.6 * math.exp(-0.3 * layer)


def _rope_tables(seq):
    half = C_HEAD_DIM // 2
    inv = ROPE_THETA ** (-jnp.arange(0, C_HEAD_DIM, 2, dtype=F32) / C_HEAD_DIM)
    ang = jnp.arange(seq, dtype=F32)[:, None] * inv[None, :]
    cos = jnp.tile(jnp.cos(ang), (1, LANES // half))
    sin = jnp.tile(jnp.sin(ang), (1, LANES // half))
    first_half = (jnp.arange(LANES) % C_HEAD_DIM) < half
    return cos, jnp.where(first_half, -sin, 0.0), jnp.where(first_half, 0.0, sin)


def kernel(x, norm_g, ev_w_in, ev_w_out, a_ln_g, a_ws, a_bs, b_conv_w, b_conv_b, b_wq, b_wk, b_wv,
           b_ig_b, b_fg_b, b_gn_g, b_skip, od_w_in, od_w_out, c_lam_q1, c_lam_k1, c_lam_q2,
           c_lam_k2, c_subln_g, final_g):
    batch, seq, d = x.shape
    depth = norm_g.shape[0]
    assert d == D_MODEL and seq % PROJ_TM == 0 and (batch * seq) % OUT_TM == 0
    x2 = x.reshape(batch * seq, d)
    cos, sin_lo, sin_hi = _rope_tables(seq)
    fg = final_g.reshape(1, d)
    row = lambda a: a.reshape(1, -1)
    for layer in range(depth):
        g = row(norm_g[layer])
        final = layer == depth - 1
        if layer % 2 == 0:
            e = layer // 2
            w_in = ev_w_in[e]
            w_main = jnp.concatenate([w_in[:, :GATE_LO], w_in[:, GATE_LO + 2 * B_HEADS:]], axis=1).astype(BF16)
            w_gate = jnp.pad(w_in[:, GATE_LO:GATE_LO + 2 * B_HEADS], ((0, 0), (0, LANES - 2 * B_HEADS))).astype(BF16)
            proj, gates = _even_proj(x2, g, w_main, w_gate)
            gbias = jnp.pad(jnp.concatenate([b_ig_b[e], b_fg_b[e]]), (0, LANES - 2 * B_HEADS)).reshape(1, LANES)
            y = _even_mixer(proj, gates, row(a_ln_g[e]), a_ws[e].astype(BF16), a_bs[e].T, b_conv_w[e],
                            row(b_conv_b[e]), b_wq[e].astype(BF16), b_wk[e].astype(BF16),
                            b_wv[e].astype(BF16), gbias, row(b_gn_g[e]), row(b_skip[e]), batch, seq)
            x2 = _out_proj(y, ev_w_out[e].astype(BF16), x2, fg, final)
        else:
            o = layer // 2
            proj = _odd_proj(x2, g, od_w_in[o].astype(BF16), cos, sin_lo, sin_hi, seq)
            lamp = jnp.stack([c_lam_q1[o], c_lam_k1[o], c_lam_q2[o], c_lam_k2[o]])
            y = _attention(proj, lamp, row(c_subln_g[o]), batch, seq, _lambda_init(layer))
            x2 = _out_proj(y, od_w_out[o].astype(BF16), x2, fg, final)
    return x2.reshape(batch, seq, d)
```

```python
import functools
import math

import jax
import jax.numpy as jnp
from jax import lax
from jax.experimental import pallas as pl
from jax.experimental.pallas import tpu as pltpu

F32 = jnp.float32
BF16 = jnp.bfloat16

D_MODEL = 1024
A_GROUPS = 8
CHUNK = 128
B_HEADS = 4
B_HEAD_DIM = 256
B_CONV = 4
C_HEADS = 16
C_HEAD_DIM = 64
LANES = 128
ROPE_THETA = 10000.0
NORM_EPS = 1e-6
LOG2E = 1.4426950408889634
EVEN_MAIN = 6 * D_MODEL
ODD_IN = 8 * D_MODEL
GATE_LO = 5 * D_MODEL

VMEM_LIMIT_BYTES = 56 * 1024 * 1024


def _params(semantics):
    return pltpu.CompilerParams(dimension_semantics=semantics, vmem_limit_bytes=VMEM_LIMIT_BYTES)


def _rmsnorm_rows(x, g):
    return x * lax.rsqrt(jnp.mean(x * x, axis=-1, keepdims=True) + NORM_EPS) * g


def _layernorm_rows(x, g):
    xc = x - jnp.mean(x, axis=-1, keepdims=True)
    return xc * lax.rsqrt(jnp.mean(xc * xc, axis=-1, keepdims=True) + NORM_EPS) * g


def _dot(a, b):
    return jnp.dot(a, b, preferred_element_type=F32)


def _dot_nt(a, b):
    return lax.dot_general(a, b, (((1,), (1,)), ((), ())), preferred_element_type=F32)


def _dot_tn(a, b):
    return lax.dot_general(a, b, (((0,), (0,)), ((), ())), preferred_element_type=F32)


PROJ_TM = 512
PROJ_TN = 512


def _even_proj_kernel(x_ref, g_ref, w_ref, wg_ref, o_ref, gate_ref):
    hn = _rmsnorm_rows(x_ref[...], g_ref[...]).astype(BF16)
    gate_ref[...] = _dot(hn, wg_ref[...])
    for c in range(EVEN_MAIN // PROJ_TN):
        cols = slice(c * PROJ_TN, (c + 1) * PROJ_TN)
        o_ref[:, cols] = _dot(hn, w_ref[:, cols]).astype(BF16)


def _odd_proj_kernel(x_ref, g_ref, w_ref, cos_ref, sin_lo_ref, sin_hi_ref, o_ref):
    hn = _rmsnorm_rows(x_ref[...], g_ref[...]).astype(BF16)
    cos, sin_lo, sin_hi = cos_ref[...], sin_lo_ref[...], sin_hi_ref[...]
    q_cols = C_HEADS * 2 * C_HEAD_DIM
    for c in range(ODD_IN // PROJ_TN):
        cols = slice(c * PROJ_TN, (c + 1) * PROJ_TN)
        acc = _dot(hn, w_ref[:, cols])
        if c * PROJ_TN < 2 * q_cols:
            scale = C_HEAD_DIM ** -0.5 * LOG2E if c * PROJ_TN < q_cols else 1.0
            for j in range(PROJ_TN // LANES):
                blk = acc[:, j * LANES:(j + 1) * LANES]
                rot = (blk * cos + pltpu.roll(blk, LANES - C_HEAD_DIM // 2, 1) * sin_lo
                       + pltpu.roll(blk, C_HEAD_DIM // 2, 1) * sin_hi)
                o_ref[:, c * PROJ_TN + j * LANES:c * PROJ_TN + (j + 1) * LANES] = (rot * scale).astype(BF16)
        else:
            o_ref[:, cols] = acc.astype(BF16)


def _even_proj(x2, g, w_main, w_gate):
    t = x2.shape[0]
    return pl.pallas_call(
        _even_proj_kernel,
        grid=(t // PROJ_TM,),
        in_specs=[
            pl.BlockSpec((PROJ_TM, D_MODEL), lambda i: (i, 0)),
            pl.BlockSpec((1, D_MODEL), lambda i: (0, 0)),
            pl.BlockSpec((D_MODEL, EVEN_MAIN), lambda i: (0, 0), pipeline_mode=pl.Buffered(1)),
            pl.BlockSpec((D_MODEL, LANES), lambda i: (0, 0)),
        ],
        out_specs=[
            pl.BlockSpec((PROJ_TM, EVEN_MAIN), lambda i: (i, 0)),
            pl.BlockSpec((PROJ_TM, LANES), lambda i: (i, 0)),
        ],
        out_shape=[
            jax.ShapeDtypeStruct((t, EVEN_MAIN), BF16),
            jax.ShapeDtypeStruct((t, LANES), F32),
        ],
        compiler_params=_params(("parallel",)),
        name="even_proj",
    )(x2, g, w_main, w_gate)


def _odd_proj(x2, g, w, cos, sin_lo, sin_hi, seq):
    t = x2.shape[0]
    pos_blocks = seq // PROJ_TM
    tab = pl.BlockSpec((PROJ_TM, LANES), lambda i: (i % pos_blocks, 0))
    return pl.pallas_call(
        _odd_proj_kernel,
        grid=(t // PROJ_TM,),
        in_specs=[
            pl.BlockSpec((PROJ_TM, D_MODEL), lambda i: (i, 0)),
            pl.BlockSpec((1, D_MODEL), lambda i: (0, 0)),
            pl.BlockSpec((D_MODEL, ODD_IN), lambda i: (0, 0), pipeline_mode=pl.Buffered(1)),
            tab, tab, tab,
        ],
        out_specs=pl.BlockSpec((PROJ_TM, ODD_IN), lambda i: (i, 0)),
        out_shape=jax.ShapeDtypeStruct((t, ODD_IN), BF16),
        compiler_params=_params(("parallel",)),
        name="odd_proj",
    )(x2, g, w, cos, sin_lo, sin_hi)


OUT_TM = 512


def _out_proj_kernel(y_ref, w_ref, x_ref, fg_ref, o_ref, *, final):
    acc = _dot(y_ref[...], w_ref[...]) + x_ref[...]
    if final:
        acc = _rmsnorm_rows(acc, fg_ref[...])
    o_ref[...] = acc


def _out_proj(y, w, x2, final_g, final):
    t, k = y.shape
    return pl.pallas_call(
        functools.partial(_out_proj_kernel, final=final),
        grid=(t // OUT_TM,),
        in_specs=[
            pl.BlockSpec((OUT_TM, k), lambda i: (i, 0)),
            pl.BlockSpec((k, D_MODEL), lambda i: (0, 0)),
            pl.BlockSpec((OUT_TM, D_MODEL), lambda i: (i, 0)),
            pl.BlockSpec((1, D_MODEL), lambda i: (0, 0)),
        ],
        out_specs=pl.BlockSpec((OUT_TM, D_MODEL), lambda i: (i, 0)),
        out_shape=jax.ShapeDtypeStruct((t, D_MODEL), F32),
        compiler_params=_params(("parallel",)),
        name="out_proj_final" if final else "out_proj",
    )(y, w, x2, final_g)


MIX_BATCH = 4
MIX_UNITS = 16
GELU_C0 = math.sqrt(2.0 / math.pi)
GELU_C1 = 0.044715 * GELU_C0


def _gelu_tanh(x):
    hx = 0.5 * x
    return hx + hx * jnp.tanh(x * (GELU_C0 + GELU_C1 * (x * x)))


def _silu_tanh(x):
    hx = 0.5 * x
    return hx + hx * jnp.tanh(hx)


def _even_mixer_kernel(u_ref, va_ref, za_ref, xm_ref, og_ref, zb_ref, gate_ref,
                       ln_g_ref, ws_ref, bs_ref, cw_ref, cb_ref, wq_ref, wk_ref, wv_ref,
                       gbias_ref, gn_g_ref, skip_ref,
                       o_ref, prev_ref, c_ref, n_ref, m_ref):
    @pl.when(pl.program_id(1) == 0)
    def _():
        prev_ref[...] = jnp.zeros_like(prev_ref)
        c_ref[...] = jnp.zeros_like(c_ref)
        n_ref[...] = jnp.zeros_like(n_ref)
        m_ref[...] = jnp.zeros_like(m_ref)

    seqs = range(u_ref.shape[0])
    row = lax.broadcasted_iota(jnp.int32, (CHUNK, CHUNK), 0)
    col = lax.broadcasted_iota(jnp.int32, (CHUNK, CHUNK), 1)
    causal = col <= row

    w_s = [jnp.where(causal, ws_ref[g], jnp.zeros((), BF16)) for g in range(A_GROUPS)]
    for b in seqs:
        u = _gelu_tanh(u_ref[b].astype(F32))
        va = _layernorm_rows(_gelu_tanh(va_ref[b].astype(F32)), ln_g_ref[...]).astype(BF16)
        za = za_ref[b].astype(F32)
        for g in range(A_GROUPS):
            cols = slice(g * CHUNK, (g + 1) * CHUNK)
            sg = _dot(w_s[g], va[:, cols]) + bs_ref[:, g:g + 1]
            o_ref[b, :, cols] = (u[:, cols] * sg * _silu_tanh(za[:, cols])).astype(BF16)

    sel_row = lax.broadcasted_iota(jnp.int32, ((B_CONV - 1) * CHUNK, 2 * CHUNK), 0)
    sel_col = lax.broadcasted_iota(jnp.int32, ((B_CONV - 1) * CHUNK, 2 * CHUNK), 1)
    chunk_bits = CHUNK.bit_length() - 1
    shift = jnp.right_shift(sel_row, chunk_bits) + 1
    select = (sel_col == jnp.bitwise_and(sel_row, CHUNK - 1) + CHUNK - shift).astype(BF16)
    tri = causal.astype(BF16)
    xc, xc_b, xm_b, pre, pre_t, bcum, bcum_t = [], [], [], [], [], [], []
    for b in seqs:
        xm_b.append(xm_ref[b])
        window = jnp.concatenate([prev_ref[b], xm_b[b]], axis=0)
        prev_ref[b] = xm_b[b]
        shifted = _dot(select, window)
        conv = xm_b[b].astype(F32) * cw_ref[B_CONV - 1:B_CONV, :] + cb_ref[...]
        for j in range(1, B_CONV):
            conv = conv + shifted[(j - 1) * CHUNK:j * CHUNK, :] * cw_ref[B_CONV - 1 - j:B_CONV - j, :]
        xc.append(_silu_tanh(conv))
        xc_b.append(xc[b].astype(BF16))

        pre.append(gate_ref[b] + gbias_ref[...])
        logf = jnp.minimum(pre[b], 0.0) - jnp.log(1.0 + jnp.exp(-jnp.abs(pre[b])))
        logf_hi = logf.astype(BF16)
        logf_lo = (logf - logf_hi.astype(F32)).astype(BF16)
        bcum.append(_dot(tri, logf_hi) + _dot(tri, logf_lo))
        pre_t.append(pre[b].T)
        bcum_t.append(bcum[b].T)

    all_units = [(b, h) for b in seqs for h in range(B_HEADS)]
    hcols = [slice(h * B_HEAD_DIM, (h + 1) * B_HEAD_DIM) for h in range(B_HEADS)]
    for first in range(0, len(all_units), MIX_UNITS):
        units = all_units[first:first + MIX_UNITS]
        q = [_dot(xc_b[b][:, hcols[h]], wq_ref[h]) for b, h in units]
        k = [_dot(xc_b[b][:, hcols[h]], wk_ref[h]) * (B_HEAD_DIM ** -0.5) for b, h in units]
        v_b = [_dot(xm_b[b][:, hcols[h]], wv_ref[h]).astype(BF16) for b, h in units]
        idx = range(len(units))
        q_b = [q[i].astype(BF16) for i in idx]

        i_col = [pre[b][:, h:h + 1] for b, h in units]
        i_row = [pre_t[b][h:h + 1, :] for b, h in units]
        b_col = [bcum[b][:, B_HEADS + h:B_HEADS + h + 1] for b, h in units]
        b_row = [bcum_t[b][B_HEADS + h:B_HEADS + h + 1, :] for b, h in units]
        b_last = [b_col[i][CHUNK - 1:CHUNK, :] for i in idx]
        m_prev = [m_ref[b, h][:, 0:1] for b, h in units]
        c_prev = [c_ref[b, h] for b, h in units]
        n_prev = [n_ref[b, h] for b, h in units]

        a_col = [b_col[i] + m_prev[i] for i in idx]
        dmat = [jnp.where(causal, b_col[i] - b_row[i] + i_row[i], -jnp.inf) for i in idx]
        m_t = [jnp.maximum(a_col[i], jnp.max(dmat[i], axis=1, keepdims=True)) for i in idx]
        sc = [_dot_nt(q_b[i], k[i].astype(BF16)) * jnp.exp(dmat[i] - m_t[i]) for i in idx]
        w_inter = [jnp.exp(a_col[i] - m_t[i]) for i in idx]
        q_c = [_dot(q_b[i], c_prev[i].astype(BF16)) for i in idx]
        num = [_dot(sc[i].astype(BF16), v_b[i]) + w_inter[i] * q_c[i] for i in idx]
        den = [jnp.sum(sc[i], axis=1, keepdims=True)
               + w_inter[i] * jnp.sum(q[i] * n_prev[i], axis=1, keepdims=True) for i in idx]
        h_t = [num[i] / jnp.maximum(jnp.abs(den[i]), jnp.exp(-m_t[i])) for i in idx]

        g_col = [b_last[i] - b_col[i] + i_col[i] for i in idx]
        g_row = [b_last[i] - b_row[i] + i_row[i] for i in idx]
        m_new = [jnp.maximum(b_last[i] + m_prev[i], jnp.max(g_row[i], axis=1, keepdims=True))
                 for i in idx]
        kw = [k[i] * jnp.exp(g_col[i] - m_new[i]) for i in idx]
        decay = [jnp.exp(b_last[i] + m_prev[i] - m_new[i]) for i in idx]
        for i, (b, h) in enumerate(units):
            c_ref[b, h] = decay[i] * c_prev[i] + _dot_tn(kw[i].astype(BF16), v_b[i])
        for i, (b, h) in enumerate(units):
            n_ref[b, h] = decay[i] * n_prev[i] + jnp.sum(kw[i], axis=0, keepdims=True)
            m_ref[b, h] = jnp.broadcast_to(m_new[i], (1, LANES))

        for i, (b, h) in enumerate(units):
            og = og_ref[b, :, hcols[h]].astype(F32)
            zb = zb_ref[b, :, hcols[h]].astype(F32)
            gated = h_t[i] * (0.5 + 0.5 * jnp.tanh(0.5 * og))
            h_n = _layernorm_rows(gated, gn_g_ref[:, hcols[h]])
            y_b = (h_n + skip_ref[:, hcols[h]] * xc[b][:, hcols[h]]) * _silu_tanh(zb)
            o_ref[b, :, D_MODEL + h * B_HEAD_DIM:D_MODEL + (h + 1) * B_HEAD_DIM] = y_b.astype(BF16)


def _even_mixer(proj, gates, ln_g, ws, bs_t, cw, cb, wq, wk, wv, gbias, gn_g, skip, batch, seq):
    t = proj.shape[0]
    nc = seq // CHUNK
    mb = MIX_BATCH if batch % MIX_BATCH == 0 else 1
    proj3 = proj.reshape(batch, seq, EVEN_MAIN)
    gates3 = gates.reshape(batch, seq, LANES)

    def col_block(j):
        return pl.BlockSpec((mb, CHUNK, D_MODEL), lambda b, c, j=j: (b, c, j))

    def whole(a):
        nd = a.ndim
        return pl.BlockSpec(a.shape, lambda b, c, nd=nd: (0,) * nd)

    consts = (ln_g, ws, bs_t, cw, cb, wq, wk, wv, gbias, gn_g, skip)
    out = pl.pallas_call(
        _even_mixer_kernel,
        grid=(batch // mb, nc),
        in_specs=[col_block(j) for j in range(6)]
        + [pl.BlockSpec((mb, CHUNK, LANES), lambda b, c: (b, c, 0))]
        + [whole(a) for a in consts],
        out_specs=pl.BlockSpec((mb, CHUNK, 2 * D_MODEL), lambda b, c: (b, c, 0)),
        out_shape=jax.ShapeDtypeStruct((batch, seq, 2 * D_MODEL), BF16),
        scratch_shapes=[
            pltpu.VMEM((mb, CHUNK, D_MODEL), BF16),
            pltpu.VMEM((mb, B_HEADS, B_HEAD_DIM, B_HEAD_DIM), F32),
            pltpu.VMEM((mb, B_HEADS, 1, B_HEAD_DIM), F32),
            pltpu.VMEM((mb, B_HEADS, 1, LANES), F32),
        ],
        compiler_params=_params(("parallel", "arbitrary")),
        name="even_mixer",
    )(proj3, proj3, proj3, proj3, proj3, proj3, gates3, *consts)
    return out.reshape(t, 2 * D_MODEL)


ATTN_TILE = 512
ATTN_ROWS = 64
ATTN_BAND = 256
ATTN_OFF_ROWS = 256
ATTN_UNROLL = 12


def _attn_kernel(lamp_ref, subln_ref, q_ref, k_ref, v_ref, z_ref, o_ref,
                 vext_ref, qc_ref, p0_ref, p1_ref, alpha0_ref, alpha1_ref, m_ref, acc_ref,
                 *, tile, lam_init):
    p_refs, alpha_refs = (p0_ref, p1_ref), (alpha0_ref, alpha1_ref)
    seq = q_ref.shape[0]
    nq = seq // tile
    rows = min(ATTN_ROWS, tile)
    half = min(ATTN_BAND, tile)
    off_rows = min(ATTN_OFF_ROWS, tile)

    lane = lax.broadcasted_iota(jnp.int32, (tile, LANES), 1)
    zero = jnp.zeros((), BF16)

    def start(i):
        return pl.multiple_of(jnp.asarray(i, jnp.int32) * tile, tile)

    def prepare_tile(i):
        at = pl.ds(start(i), tile)
        vext_ref[at, :LANES] = v_ref[at, :]
        vext_ref[at, LANES:] = jnp.ones((tile, LANES), BF16)
        q = q_ref[at, :]
        qc_ref[0, at, :] = jnp.where(lane < C_HEAD_DIM, q, zero)
        qc_ref[1, at, :] = jnp.where(lane >= C_HEAD_DIM, q, zero)

    bands = [(r0, r0 + half) for r0 in range(0, tile, half)]

    diag_units = [(c, r0, half, keys) for c in range(2) for r0, keys in bands]
    off_units = [(c, r0, off_rows, tile) for c in range(2) for r0 in range(0, tile, off_rows)]

    def softmax_unit(qi, j, slot, unit, diagonal):
        c, r0, nrows, keys = unit
        k = k_ref[pl.ds(start(j), keys), :]
        s_all = _dot_nt(qc_ref[c, pl.ds(start(qi) + r0, nrows), :], k)
        for r in range(0, nrows, rows):
            dst = slice(r0 + r, r0 + r + rows)
            s = s_all[r:r + rows, :]
            if diagonal:
                keep = (lax.broadcasted_iota(jnp.int32, (rows, keys), 1)
                        <= lax.broadcasted_iota(jnp.int32, (rows, keys), 0) + (r0 + r))
                s = jnp.where(keep, s, -jnp.inf)
                m_new = jnp.broadcast_to(jnp.max(s, axis=1, keepdims=True), (rows, LANES))
            else:
                m_prev = m_ref[qi, c, dst, :]
                m_new = jnp.maximum(m_prev, jnp.max(s, axis=1, keepdims=True))
                alpha_refs[slot][c, dst, :] = jnp.exp2(m_prev - m_new)
            p = jnp.exp2(s - jnp.concatenate([m_new] * (keys // LANES), axis=1))
            p_refs[slot][c, dst, :keys] = p.astype(BF16)
            m_ref[qi, c, dst, :] = m_new

    def values_unit(qi, j, slot, unit, diagonal):
        c, r0, nrows, keys = unit
        dst = slice(r0, r0 + nrows)
        pv = _dot(p_refs[slot][c, dst, :keys], vext_ref[pl.ds(start(j), keys), :])
        if diagonal:
            acc_ref[qi, c, dst, :] = pv
        else:
            alpha = alpha_refs[slot][c, dst, :]
            acc_ref[qi, c, dst, :] = jnp.concatenate([alpha, alpha], axis=1) * acc_ref[qi, c, dst, :] + pv

    def pipeline(blocks, diagonal, unroll, after_tail_step=None):
        nblk = len(blocks)
        units = diag_units if diagonal else off_units

        def step(cur, prev, t, prepare=None):
            for group in ([units] if diagonal else [[unit] for unit in units]):
                for unit in group:
                    if cur is not None:
                        softmax_unit(*cur, t % 2, unit, diagonal)
                for unit in group:
                    if prev is not None:
                        values_unit(*prev, (t - 1) % 2, unit, diagonal)
            if prepare is not None:
                prepare_tile(prepare)

        def static_step(t):
            step(blocks[t] if t < nblk else None, blocks[t - 1] if 1 <= t <= nblk else None, t,
                 t + 1 if diagonal and t + 1 < nblk else None)

        def next_block(qi, j):
            if diagonal:
                return qi + 1, j + 1
            wrap = j + 1 == qi
            return jnp.where(wrap, qi + 1, qi), jnp.where(wrap, 0, j + 1)

        if diagonal:
            prepare_tile(0)
        static_step(0)
        n_trips = max(nblk - (2 if diagonal else 1), 0) // unroll
        if n_trips > 0:
            def body(i, carry):
                prev, cur = carry
                for u in range(unroll):
                    step(cur, prev, 1 + u, cur[0] + 1 if diagonal else None)
                    prev, cur = cur, next_block(*cur)
                return prev, cur

            as_i32 = lambda b: (jnp.int32(b[0]), jnp.int32(b[1]))
            lax.fori_loop(0, n_trips, body, (as_i32(blocks[0]), as_i32(blocks[1])))
        for t in range(1 + unroll * n_trips, nblk + 1):
            static_step(t)
            if after_tail_step is not None:
                after_tail_step(t, unroll * n_trips, nblk)

    lp = lamp_ref[...]
    lam = (jnp.exp(jnp.sum(lp[0:1] * lp[1:2], axis=-1, keepdims=True))
           - jnp.exp(jnp.sum(lp[2:3] * lp[3:4], axis=-1, keepdims=True)) + lam_init)

    subln_gain = subln_ref[...] * (1.0 - lam_init)

    def finalize(qi, carry):
        a0, a1 = acc_ref[qi, 0], acc_ref[qi, 1]
        o = a0[:, :LANES] / a0[:, LANES:] - lam * (a1[:, :LANES] / a1[:, LANES:])
        z = z_ref[pl.ds(start(qi), tile), :].astype(F32)
        o_ref[pl.ds(start(qi), tile), :] = (_rmsnorm_rows(o, subln_gain) * _silu_tanh(z)).astype(BF16)
        return carry

    pipeline([(qi, qi) for qi in range(nq)], True, 2)
    if nq == 1:
        finalize(0, 0)
        return

    done_at = {0: 0}
    for qi in range(1, nq):
        done_at[qi] = qi * (qi + 1) // 2
    finished = set()

    def finalize_ready(t, loop_steps, nblk):
        tail_steps = nblk - loop_steps
        ready = [qi for qi in range(nq) if done_at[qi] <= max(t, loop_steps) and qi not in finished]
        already = [qi for qi in ready if done_at[qi] <= loop_steps]
        share = -(-sum(done_at[qi] <= loop_steps for qi in range(nq)) // tail_steps)
        for qi in [qi for qi in ready if done_at[qi] > loop_steps] + already[:share]:
            finalize(qi, 0)
            finished.add(qi)

    pipeline([(qi, j) for qi in range(1, nq) for j in range(qi)], False, ATTN_UNROLL, finalize_ready)
    for qi in range(nq):
        if qi not in finished:
            finalize(qi, 0)


def _attention(proj, lamp, subln, batch, seq, lam_init):
    t = proj.shape[0]
    tile = min(ATTN_TILE, seq)
    nq = seq // tile

    def head_block(part):
        return pl.BlockSpec((seq, LANES), lambda b, h, part=part: (b, part * C_HEADS + h))

    slots = lambda shape, dtype: [pltpu.VMEM(shape, dtype), pltpu.VMEM(shape, dtype)]
    return pl.pallas_call(
        functools.partial(_attn_kernel, tile=tile, lam_init=lam_init),
        grid=(batch, C_HEADS),
        in_specs=[
            pl.BlockSpec(lamp.shape, lambda b, h: (0, 0)),
            pl.BlockSpec(subln.shape, lambda b, h: (0, 0)),
            head_block(0), head_block(1), head_block(2), head_block(3),
        ],
        out_specs=pl.BlockSpec((seq, LANES), lambda b, h: (b, h)),
        out_shape=jax.ShapeDtypeStruct((t, C_HEADS * LANES), BF16),
        scratch_shapes=[pltpu.VMEM((seq, 2 * LANES), BF16),
                        pltpu.VMEM((2, seq, LANES), BF16)]
        + slots((2, tile, tile), BF16)
        + slots((2, tile, LANES), F32)
        + [pltpu.VMEM((nq, 2, tile, LANES), F32),
           pltpu.VMEM((nq, 2, tile, 2 * LANES), F32)],
        compiler_params=_params(("parallel", "parallel")),
        name="diff_attention",
    )(lamp, subln, proj, proj, proj, proj)


def _lambda_init(layer):
    return 0.8 - 0.6 * math.exp(-0.3 * layer)


def _rope_tables(seq):
    half = C_HEAD_DIM // 2
    inv = ROPE_THETA ** (-jnp.arange(0, C_HEAD_DIM, 2, dtype=F32) / C_HEAD_DIM)
    ang = jnp.arange(seq, dtype=F32)[:, None] * inv[None, :]
    cos = jnp.tile(jnp.cos(ang), (1, LANES // half))
    sin = jnp.tile(jnp.sin(ang), (1, LANES // half))
    first_half = (jnp.arange(LANES) % C_HEAD_DIM) < half
    return cos, jnp.where(first_half, -sin, 0.0), jnp.where(first_half, 0.0, sin)


def kernel(x, norm_g, ev_w_in, ev_w_out, a_ln_g, a_ws, a_bs, b_conv_w, b_conv_b, b_wq, b_wk, b_wv,
           b_ig_b, b_fg_b, b_gn_g, b_skip, od_w_in, od_w_out, c_lam_q1, c_lam_k1, c_lam_q2,
           c_lam_k2, c_subln_g, final_g):
    batch, seq, d = x.shape
    depth = norm_g.shape[0]
    assert d == D_MODEL and seq % PROJ_TM == 0 and (batch * seq) % OUT_TM == 0
    x2 = x.reshape(batch * seq, d)
    cos, sin_lo, sin_hi = _rope_tables(seq)
    fg = final_g.reshape(1, d)
    row = lambda a: a.reshape(1, -1)
    for layer in range(depth):
        g = row(norm_g[layer])
        final = layer == depth - 1
        if layer % 2 == 0:
            e = layer // 2
            w_in = ev_w_in[e]
            w_main = jnp.concatenate([w_in[:, :GATE_LO], w_in[:, GATE_LO + 2 * B_HEADS:]], axis=1).astype(BF16)
            w_gate = jnp.pad(w_in[:, GATE_LO:GATE_LO + 2 * B_HEADS], ((0, 0), (0, LANES - 2 * B_HEADS))).astype(BF16)
            proj, gates = _even_proj(x2, g, w_main, w_gate)
            gbias = jnp.pad(jnp.concatenate([b_ig_b[e], b_fg_b[e]]), (0, LANES - 2 * B_HEADS)).reshape(1, LANES)
            y = _even_mixer(proj, gates, row(a_ln_g[e]), a_ws[e].astype(BF16), a_bs[e].T, b_conv_w[e],
                            row(b_conv_b[e]), b_wq[e].astype(BF16), b_wk[e].astype(BF16),
                            b_wv[e].astype(BF16), gbias, row(b_gn_g[e]), row(b_skip[e]), batch, seq)
            x2 = _out_proj(y, ev_w_out[e].astype(BF16), x2, fg, final)
        else:
            o = layer // 2
            proj = _odd_proj(x2, g, od_w_in[o].astype(BF16), cos, sin_lo, sin_hi, seq)
            lamp = jnp.stack([c_lam_q1[o], c_lam_k1[o], c_lam_q2[o], c_lam_k2[o]])
            y = _attention(proj, lamp, row(c_subln_g[o]), batch, seq, _lambda_init(layer))
            x2 = _out_proj(y, od_w_out[o].astype(BF16), x2, fg, final)
    return x2.reshape(batch, seq, d)
```

```python
import functools
import math

import jax
import jax.numpy as jnp
from jax import lax
from jax.experimental import pallas as pl
from jax.experimental.pallas import tpu as pltpu

F32 = jnp.float32
BF16 = jnp.bfloat16

D_MODEL = 1024
A_GROUPS = 8
CHUNK = 128
B_HEADS = 4
B_HEAD_DIM = 256
B_CONV = 4
C_HEADS = 16
C_HEAD_DIM = 64
LANES = 128
ROPE_THETA = 10000.0
NORM_EPS = 1e-6
LOG2E = 1.4426950408889634
EVEN_MAIN = 6 * D_MODEL
ODD_IN = 8 * D_MODEL
GATE_LO = 5 * D_MODEL

VMEM_LIMIT_BYTES = 56 * 1024 * 1024


def _params(semantics):
    return pltpu.CompilerParams(dimension_semantics=semantics, vmem_limit_bytes=VMEM_LIMIT_BYTES)


def _rmsnorm_rows(x, g):
    return x * lax.rsqrt(jnp.mean(x * x, axis=-1, keepdims=True) + NORM_EPS) * g


def _layernorm_rows(x, g):
    xc = x - jnp.mean(x, axis=-1, keepdims=True)
    return xc * lax.rsqrt(jnp.mean(xc * xc, axis=-1, keepdims=True) + NORM_EPS) * g


def _dot(a, b):
    return jnp.dot(a, b, preferred_element_type=F32)


def _dot_nt(a, b):
    return lax.dot_general(a, b, (((1,), (1,)), ((), ())), preferred_element_type=F32)


def _dot_tn(a, b):
    return lax.dot_general(a, b, (((0,), (0,)), ((), ())), preferred_element_type=F32)


PROJ_TM = 512
PROJ_TN = 512


def _even_proj_kernel(x_ref, g_ref, w_ref, wg_ref, o_ref, gate_ref):
    hn = _rmsnorm_rows(x_ref[...], g_ref[...]).astype(BF16)
    gate_ref[...] = _dot(hn, wg_ref[...])
    for c in range(EVEN_MAIN // PROJ_TN):
        cols = slice(c * PROJ_TN, (c + 1) * PROJ_TN)
        o_ref[:, cols] = _dot(hn, w_ref[:, cols]).astype(BF16)


def _odd_proj_kernel(x_ref, g_ref, w_ref, cos_ref, sin_lo_ref, sin_hi_ref, o_ref):
    hn = _rmsnorm_rows(x_ref[...], g_ref[...]).astype(BF16)
    cos, sin_lo, sin_hi = cos_ref[...], sin_lo_ref[...], sin_hi_ref[...]
    q_cols = C_HEADS * 2 * C_HEAD_DIM
    for c in range(ODD_IN // PROJ_TN):
        cols = slice(c * PROJ_TN, (c + 1) * PROJ_TN)
        acc = _dot(hn, w_ref[:, cols])
        if c * PROJ_TN < 2 * q_cols:
            scale = C_HEAD_DIM ** -0.5 * LOG2E if c * PROJ_TN < q_cols else 1.0
            for j in range(PROJ_TN // LANES):
                blk = acc[:, j * LANES:(j + 1) * LANES]
                rot = (blk * cos + pltpu.roll(blk, LANES - C_HEAD_DIM // 2, 1) * sin_lo
                       + pltpu.roll(blk, C_HEAD_DIM // 2, 1) * sin_hi)
                o_ref[:, c * PROJ_TN + j * LANES:c * PROJ_TN + (j + 1) * LANES] = (rot * scale).astype(BF16)
        else:
            o_ref[:, cols] = acc.astype(BF16)


def _even_proj(x2, g, w_main, w_gate):
    t = x2.shape[0]
    return pl.pallas_call(
        _even_proj_kernel,
        grid=(t // PROJ_TM,),
        in_specs=[
            pl.BlockSpec((PROJ_TM, D_MODEL), lambda i: (i, 0)),
            pl.BlockSpec((1, D_MODEL), lambda i: (0, 0)),
            pl.BlockSpec((D_MODEL, EVEN_MAIN), lambda i: (0, 0), pipeline_mode=pl.Buffered(1)),
            pl.BlockSpec((D_MODEL, LANES), lambda i: (0, 0)),
        ],
        out_specs=[
            pl.BlockSpec((PROJ_TM, EVEN_MAIN), lambda i: (i, 0)),
            pl.BlockSpec((PROJ_TM, LANES), lambda i: (i, 0)),
        ],
        out_shape=[
            jax.ShapeDtypeStruct((t, EVEN_MAIN), BF16),
            jax.ShapeDtypeStruct((t, LANES), F32),
        ],
        compiler_params=_params(("parallel",)),
        name="even_proj",
    )(x2, g, w_main, w_gate)


def _odd_proj(x2, g, w, cos, sin_lo, sin_hi, seq):
    t = x2.shape[0]
    pos_blocks = seq // PROJ_TM
    tab = pl.BlockSpec((PROJ_TM, LANES), lambda i: (i % pos_blocks, 0))
    return pl.pallas_call(
        _odd_proj_kernel,
        grid=(t // PROJ_TM,),
        in_specs=[
            pl.BlockSpec((PROJ_TM, D_MODEL), lambda i: (i, 0)),
            pl.BlockSpec((1, D_MODEL), lambda i: (0, 0)),
            pl.BlockSpec((D_MODEL, ODD_IN), lambda i: (0, 0), pipeline_mode=pl.Buffered(1)),
            tab, tab, tab,
        ],
        out_specs=pl.BlockSpec((PROJ_TM, ODD_IN), lambda i: (i, 0)),
        out_shape=jax.ShapeDtypeStruct((t, ODD_IN), BF16),
        compiler_params=_params(("parallel",)),
        name="odd_proj",
    )(x2, g, w, cos, sin_lo, sin_hi)


OUT_TM = 512


def _out_proj_kernel(y_ref, w_ref, x_ref, fg_ref, o_ref, *, final):
    acc = _dot(y_ref[...], w_ref[...]) + x_ref[...]
    if final:
        acc = _rmsnorm_rows(acc, fg_ref[...])
    o_ref[...] = acc


def _out_proj(y, w, x2, final_g, final):
    t, k = y.shape
    return pl.pallas_call(
        functools.partial(_out_proj_kernel, final=final),
        grid=(t // OUT_TM,),
        in_specs=[
            pl.BlockSpec((OUT_TM, k), lambda i: (i, 0)),
            pl.BlockSpec((k, D_MODEL), lambda i: (0, 0)),
            pl.BlockSpec((OUT_TM, D_MODEL), lambda i: (i, 0)),
            pl.BlockSpec((1, D_MODEL), lambda i: (0, 0)),
        ],
        out_specs=pl.BlockSpec((OUT_TM, D_MODEL), lambda i: (i, 0)),
        out_shape=jax.ShapeDtypeStruct((t, D_MODEL), F32),
        compiler_params=_params(("parallel",)),
        name="out_proj_final" if final else "out_proj",
    )(y, w, x2, final_g)


MIX_BATCH = 4
MIX_UNITS = 16
GELU_C0 = math.sqrt(2.0 / math.pi)
GELU_C1 = 0.044715 * GELU_C0


def _gelu_tanh(x):
    hx = 0.5 * x
    return hx + hx * jnp.tanh(x * (GELU_C0 + GELU_C1 * (x * x)))


def _silu_tanh(x):
    hx = 0.5 * x
    return hx + hx * jnp.tanh(hx)


def _even_mixer_kernel(u_ref, va_ref, za_ref, xm_ref, og_ref, zb_ref, gate_ref,
                       ln_g_ref, ws_ref, bs_ref, cw_ref, cb_ref, wq_ref, wk_ref, wv_ref,
                       gbias_ref, gn_g_ref, skip_ref,
                       o_ref, prev_ref, c_ref, n_ref, m_ref):
    @pl.when(pl.program_id(1) == 0)
    def _():
        prev_ref[...] = jnp.zeros_like(prev_ref)
        c_ref[...] = jnp.zeros_like(c_ref)
        n_ref[...] = jnp.zeros_like(n_ref)
        m_ref[...] = jnp.zeros_like(m_ref)

    seqs = range(u_ref.shape[0])
    row = lax.broadcasted_iota(jnp.int32, (CHUNK, CHUNK), 0)
    col = lax.broadcasted_iota(jnp.int32, (CHUNK, CHUNK), 1)
    causal = col <= row

    w_s = [jnp.where(causal, ws_ref[g], jnp.zeros((), BF16)) for g in range(A_GROUPS)]
    for b in seqs:
        u = _gelu_tanh(u_ref[b].astype(F32))
        va = _layernorm_rows(_gelu_tanh(va_ref[b].astype(F32)), ln_g_ref[...]).astype(BF16)
        za = za_ref[b].astype(F32)
        for g in range(A_GROUPS):
            cols = slice(g * CHUNK, (g + 1) * CHUNK)
            sg = _dot(w_s[g], va[:, cols]) + bs_ref[:, g:g + 1]
            o_ref[b, :, cols] = (u[:, cols] * sg * _silu_tanh(za[:, cols])).astype(BF16)

    sel_row = lax.broadcasted_iota(jnp.int32, ((B_CONV - 1) * CHUNK, 2 * CHUNK), 0)
    sel_col = lax.broadcasted_iota(jnp.int32, ((B_CONV - 1) * CHUNK, 2 * CHUNK), 1)
    chunk_bits = CHUNK.bit_length() - 1
    shift = jnp.right_shift(sel_row, chunk_bits) + 1
    select = (sel_col == jnp.bitwise_and(sel_row, CHUNK - 1) + CHUNK - shift).astype(BF16)
    tri = causal.astype(BF16)
    xc, xc_b, xm_b, pre, pre_t, bcum, bcum_t = [], [], [], [], [], [], []
    for b in seqs:
        xm_b.append(xm_ref[b])
        window = jnp.concatenate([prev_ref[b], xm_b[b]], axis=0)
        prev_ref[b] = xm_b[b]
        shifted = _dot(select, window)
        conv = xm_b[b].astype(F32) * cw_ref[B_CONV - 1:B_CONV, :] + cb_ref[...]
        for j in range(1, B_CONV):
            conv = conv + shifted[(j - 1) * CHUNK:j * CHUNK, :] * cw_ref[B_CONV - 1 - j:B_CONV - j, :]
        xc.append(_silu_tanh(conv))
        xc_b.append(xc[b].astype(BF16))

        pre.append(gate_ref[b] + gbias_ref[...])
        logf = jnp.minimum(pre[b], 0.0) - jnp.log(1.0 + jnp.exp(-jnp.abs(pre[b])))
        logf_hi = logf.astype(BF16)
        logf_lo = (logf - logf_hi.astype(F32)).astype(BF16)
        bcum.append(_dot(tri, logf_hi) + _dot(tri, logf_lo))
        pre_t.append(pre[b].T)
        bcum_t.append(bcum[b].T)

    all_units = [(b, h) for b in seqs for h in range(B_HEADS)]
    hcols = [slice(h * B_HEAD_DIM, (h + 1) * B_HEAD_DIM) for h in range(B_HEADS)]
    for first in range(0, len(all_units), MIX_UNITS):
        units = all_units[first:first + MIX_UNITS]
        q = [_dot(xc_b[b][:, hcols[h]], wq_ref[h]) for b, h in units]
        k = [_dot(xc_b[b][:, hcols[h]], wk_ref[h]) * (B_HEAD_DIM ** -0.5) for b, h in units]
        v_b = [_dot(xm_b[b][:, hcols[h]], wv_ref[h]).astype(BF16) for b, h in units]
        idx = range(len(units))
        q_b = [q[i].astype(BF16) for i in idx]

        i_col = [pre[b][:, h:h + 1] for b, h in units]
        i_row = [pre_t[b][h:h + 1, :] for b, h in units]
        b_col = [bcum[b][:, B_HEADS + h:B_HEADS + h + 1] for b, h in units]
        b_row = [bcum_t[b][B_HEADS + h:B_HEADS + h + 1, :] for b, h in units]
        b_last = [b_col[i][CHUNK - 1:CHUNK, :] for i in idx]
        m_prev = [m_ref[b, h][:, 0:1] for b, h in units]
        c_prev = [c_ref[b, h] for b, h in units]
        n_prev = [n_ref[b, h] for b, h in units]

        a_col = [b_col[i] + m_prev[i] for i in idx]
        dmat = [jnp.where(causal, b_col[i] - b_row[i] + i_row[i], -jnp.inf) for i in idx]
        m_t = [jnp.maximum(a_col[i], jnp.max(dmat[i], axis=1, keepdims=True)) for i in idx]
        sc = [_dot_nt(q_b[i], k[i].astype(BF16)) * jnp.exp(dmat[i] - m_t[i]) for i in idx]
        w_inter = [jnp.exp(a_col[i] - m_t[i]) for i in idx]
        q_c = [_dot(q_b[i], c_prev[i].astype(BF16)) for i in idx]
        num = [_dot(sc[i].astype(BF16), v_b[i]) + w_inter[i] * q_c[i] for i in idx]
        den = [jnp.sum(sc[i], axis=1, keepdims=True)
               + w_inter[i] * jnp.sum(q[i] * n_prev[i], axis=1, keepdims=True) for i in idx]
        h_t = [num[i] / jnp.maximum(jnp.abs(den[i]), jnp.exp(-m_t[i])) for i in idx]

        g_col = [b_last[i] - b_col[i] + i_col[i] for i in idx]
        g_row = [b_last[i] - b_row[i] + i_row[i] for i in idx]
        m_new = [jnp.maximum(b_last[i] + m_prev[i], jnp.max(g_row[i], axis=1, keepdims=True))
                 for i in idx]
        kw = [k[i] * jnp.exp(g_col[i] - m_new[i]) for i in idx]
        decay = [jnp.exp(b_last[i] + m_prev[i] - m_new[i]) for i in idx]
        for i, (b, h) in enumerate(units):
            c_ref[b, h] = decay[i] * c_prev[i] + _dot_tn(kw[i].astype(BF16), v_b[i])
        for i, (b, h) in enumerate(units):
            n_ref[b, h] = decay[i] * n_prev[i] + jnp.sum(kw[i], axis=0, keepdims=True)
            m_ref[b, h] = jnp.broadcast_to(m_new[i], (1, LANES))

        for i, (b, h) in enumerate(units):
            og = og_ref[b, :, hcols[h]].astype(F32)
            zb = zb_ref[b, :, hcols[h]].astype(F32)
            gated = h_t[i] * (0.5 + 0.5 * jnp.tanh(0.5 * og))
            h_n = _layernorm_rows(gated, gn_g_ref[:, hcols[h]])
            y_b = (h_n + skip_ref[:, hcols[h]] * xc[b][:, hcols[h]]) * _silu_tanh(zb)
            o_ref[b, :, D_MODEL + h * B_HEAD_DIM:D_MODEL + (h + 1) * B_HEAD_DIM] = y_b.astype(BF16)


def _even_mixer(proj, gates, ln_g, ws, bs_t, cw, cb, wq, wk, wv, gbias, gn_g, skip, batch, seq):
    t = proj.shape[0]
    nc = seq // CHUNK
    mb = MIX_BATCH if batch % MIX_BATCH == 0 else 1
    proj3 = proj.reshape(batch, seq, EVEN_MAIN)
    gates3 = gates.reshape(batch, seq, LANES)

    def col_block(j):
        return pl.BlockSpec((mb, CHUNK, D_MODEL), lambda b, c, j=j: (b, c, j))

    def whole(a):
        nd = a.ndim
        return pl.BlockSpec(a.shape, lambda b, c, nd=nd: (0,) * nd)

    consts = (ln_g, ws, bs_t, cw, cb, wq, wk, wv, gbias, gn_g, skip)
    out = pl.pallas_call(
        _even_mixer_kernel,
        grid=(batch // mb, nc),
        in_specs=[col_block(j) for j in range(6)]
        + [pl.BlockSpec((mb, CHUNK, LANES), lambda b, c: (b, c, 0))]
        + [whole(a) for a in consts],
        out_specs=pl.BlockSpec((mb, CHUNK, 2 * D_MODEL), lambda b, c: (b, c, 0)),
        out_shape=jax.ShapeDtypeStruct((batch, seq, 2 * D_MODEL), BF16),
        scratch_shapes=[
            pltpu.VMEM((mb, CHUNK, D_MODEL), BF16),
            pltpu.VMEM((mb, B_HEADS, B_HEAD_DIM, B_HEAD_DIM), F32),
            pltpu.VMEM((mb, B_HEADS, 1, B_HEAD_DIM), F32),
            pltpu.VMEM((mb, B_HEADS, 1, LANES), F32),
        ],
        compiler_params=_params(("parallel", "arbitrary")),
        name="even_mixer",
    )(proj3, proj3, proj3, proj3, proj3, proj3, gates3, *consts)
    return out.reshape(t, 2 * D_MODEL)


ATTN_TILE = 512
ATTN_ROWS = 64
ATTN_BAND = 256
ATTN_OFF_ROWS = 256
ATTN_UNROLL = 12


def _attn_kernel(lamp_ref, subln_ref, q_ref, k_ref, v_ref, z_ref, o_ref,
                 vext_ref, qc_ref, p0_ref, p1_ref, alpha0_ref, alpha1_ref, m_ref, acc_ref,
                 *, tile, lam_init):
    p_refs, alpha_refs = (p0_ref, p1_ref), (alpha0_ref, alpha1_ref)
    seq = q_ref.shape[0]
    nq = seq // tile
    rows = min(ATTN_ROWS, tile)
    half = min(ATTN_BAND, tile)
    off_rows = min(ATTN_OFF_ROWS, tile)

    lane = lax.broadcasted_iota(jnp.int32, (tile, LANES), 1)
    zero = jnp.zeros((), BF16)

    def start(i):
        return pl.multiple_of(jnp.asarray(i, jnp.int32) * tile, tile)

    def prepare_tile(i):
        at = pl.ds(start(i), tile)
        vext_ref[at, :LANES] = v_ref[at, :]
        vext_ref[at, LANES:] = jnp.ones((tile, LANES), BF16)
        q = q_ref[at, :]
        qc_ref[0, at, :] = jnp.where(lane < C_HEAD_DIM, q, zero)
        qc_ref[1, at, :] = jnp.where(lane >= C_HEAD_DIM, q, zero)

    bands = [(r0, r0 + half) for r0 in range(0, tile, half)]

    diag_units = [(c, r0, half, keys) for c in range(2) for r0, keys in bands]
    off_units = [(c, r0, off_rows, tile) for c in range(2) for r0 in range(0, tile, off_rows)]

    def softmax_unit(qi, j, slot, unit, diagonal):
        c, r0, nrows, keys = unit
        k = k_ref[pl.ds(start(j), keys), :]
        s_all = _dot_nt(qc_ref[c, pl.ds(start(qi) + r0, nrows), :], k)
        for r in range(0, nrows, rows):
            dst = slice(r0 + r, r0 + r + rows)
            s = s_all[r:r + rows, :]
            if diagonal:
                keep = (lax.broadcasted_iota(jnp.int32, (rows, keys), 1)
                        <= lax.broadcasted_iota(jnp.int32, (rows, keys), 0) + (r0 + r))
                s = jnp.where(keep, s, -jnp.inf)
                m_new = jnp.broadcast_to(jnp.max(s, axis=1, keepdims=True), (rows, LANES))
            else:
                m_prev = m_ref[qi, c, dst, :]
                m_new = jnp.maximum(m_prev, jnp.max(s, axis=1, keepdims=True))
                alpha_refs[slot][c, dst, :] = jnp.exp2(m_prev - m_new)
            p = jnp.exp2(s - jnp.concatenate([m_new] * (keys // LANES), axis=1))
            p_refs[slot][c, dst, :keys] = p.astype(BF16)
            m_ref[qi, c, dst, :] = m_new

    def values_unit(qi, j, slot, unit, diagonal):
        c, r0, nrows, keys = unit
        dst = slice(r0, r0 + nrows)
        pv = _dot(p_refs[slot][c, dst, :keys], vext_ref[pl.ds(start(j), keys), :])
        if diagonal:
            acc_ref[qi, c, dst, :] = pv
        else:
            alpha = alpha_refs[slot][c, dst, :]
            acc_ref[qi, c, dst, :] = jnp.concatenate([alpha, alpha], axis=1) * acc_ref[qi, c, dst, :] + pv

    def pipeline(blocks, diagonal, unroll, after_tail_step=None):
        nblk = len(blocks)
        units = diag_units if diagonal else off_units

        def step(cur, prev, t, prepare=None):
            for group in ([units] if diagonal else [[unit] for unit in units]):
                for unit in group:
                    if cur is not None:
                        softmax_unit(*cur, t % 2, unit, diagonal)
                for unit in group:
                    if prev is not None:
                        values_unit(*prev, (t - 1) % 2, unit, diagonal)
            if prepare is not None:
                prepare_tile(prepare)

        def static_step(t):
            step(blocks[t] if t < nblk else None, blocks[t - 1] if 1 <= t <= nblk else None, t,
                 t + 1 if diagonal and t + 1 < nblk else None)

        def next_block(qi, j):
            if diagonal:
                return qi + 1, j + 1
            wrap = j + 1 == qi
            return jnp.where(wrap, qi + 1, qi), jnp.where(wrap, 0, j + 1)

        if diagonal:
            prepare_tile(0)
        static_step(0)
        n_trips = max(nblk - (2 if diagonal else 1), 0) // unroll
        if n_trips > 0:
            def body(i, carry):
                prev, cur = carry
                for u in range(unroll):
                    step(cur, prev, 1 + u, cur[0] + 1 if diagonal else None)
                    prev, cur = cur, next_block(*cur)
                return prev, cur

            as_i32 = lambda b: (jnp.int32(b[0]), jnp.int32(b[1]))
            lax.fori_loop(0, n_trips, body, (as_i32(blocks[0]), as_i32(blocks[1])))
        for t in range(1 + unroll * n_trips, nblk + 1):
            static_step(t)
            if after_tail_step is not None:
                after_tail_step(t, unroll * n_trips, nblk)

    lp = lamp_ref[...]
    lam = (jnp.exp(jnp.sum(lp[0:1] * lp[1:2], axis=-1, keepdims=True))
           - jnp.exp(jnp.sum(lp[2:3] * lp[3:4], axis=-1, keepdims=True)) + lam_init)

    subln_gain = subln_ref[...] * (1.0 - lam_init)

    def finalize(qi, carry):
        a0, a1 = acc_ref[qi, 0], acc_ref[qi, 1]
        o = a0[:, :LANES] / a0[:, LANES:] - lam * (a1[:, :LANES] / a1[:, LANES:])
        z = z_ref[pl.ds(start(qi), tile), :].astype(F32)
        o_ref[pl.ds(start(qi), tile), :] = (_rmsnorm_rows(o, subln_gain) * _silu_tanh(z)).astype(BF16)
        return carry

    pipeline([(qi, qi) for qi in range(nq)], True, 2 * nq)
    if nq == 1:
        finalize(0, 0)
        return

    done_at = {0: 0}
    for qi in range(1, nq):
        done_at[qi] = qi * (qi + 1) // 2
    finished = set()

    def finalize_ready(t, loop_steps, nblk):
        tail_steps = nblk - loop_steps
        ready = [qi for qi in range(nq) if done_at[qi] <= max(t, loop_steps) and qi not in finished]
        already = [qi for qi in ready if done_at[qi] <= loop_steps]
        share = -(-sum(done_at[qi] <= loop_steps for qi in range(nq)) // tail_steps)
        for qi in [qi for qi in ready if done_at[qi] > loop_steps] + already[:share]:
            finalize(qi, 0)
            finished.add(qi)

    pipeline([(qi, j) for qi in range(1, nq) for j in range(qi)], False, ATTN_UNROLL, finalize_ready)
    for qi in range(nq):
        if qi not in finished:
            finalize(qi, 0)


def _attention(proj, lamp, subln, batch, seq, lam_init):
    t = proj.shape[0]
    tile = min(ATTN_TILE, seq)
    nq = seq // tile

    def head_block(part):
        return pl.BlockSpec((seq, LANES), lambda b, h, part=part: (b, part * C_HEADS + h))

    slots = lambda shape, dtype: [pltpu.VMEM(shape, dtype), pltpu.VMEM(shape, dtype)]
    return pl.pallas_call(
        functools.partial(_attn_kernel, tile=tile, lam_init=lam_init),
        grid=(batch, C_HEADS),
        in_specs=[
            pl.BlockSpec(lamp.shape, lambda b, h: (0, 0)),
            pl.BlockSpec(subln.shape, lambda b, h: (0, 0)),
            head_block(0), head_block(1), head_block(2), head_block(3),
        ],
        out_specs=pl.BlockSpec((seq, LANES), lambda b, h: (b, h)),
        out_shape=jax.ShapeDtypeStruct((t, C_HEADS * LANES), BF16),
        scratch_shapes=[pltpu.VMEM((seq, 2 * LANES), BF16),
                        pltpu.VMEM((2, seq, LANES), BF16)]
        + slots((2, tile, tile), BF16)
        + slots((2, tile, LANES), F32)
        + [pltpu.VMEM((nq, 2, tile, LANES), F32),
           pltpu.VMEM((nq, 2, tile, 2 * LANES), F32)],
        compiler_params=_params(("parallel", "parallel")),
        name="diff_attention",
    )(lamp, subln, proj, proj, proj, proj)


def _lambda_init(layer):
    return 0.8 - 0.6 * math.exp(-0.3 * layer)


def _rope_tables(seq):
    half = C_HEAD_DIM // 2
    inv = ROPE_THETA ** (-jnp.arange(0, C_HEAD_DIM, 2, dtype=F32) / C_HEAD_DIM)
    ang = jnp.arange(seq, dtype=F32)[:, None] * inv[None, :]
    cos = jnp.tile(jnp.cos(ang), (1, LANES // half))
    sin = jnp.tile(jnp.sin(ang), (1, LANES // half))
    first_half = (jnp.arange(LANES) % C_HEAD_DIM) < half
    return cos, jnp.where(first_half, -sin, 0.0), jnp.where(first_half, 0.0, sin)


def kernel(x, norm_g, ev_w_in, ev_w_out, a_ln_g, a_ws, a_bs, b_conv_w, b_conv_b, b_wq, b_wk, b_wv,
           b_ig_b, b_fg_b, b_gn_g, b_skip, od_w_in, od_w_out, c_lam_q1, c_lam_k1, c_lam_q2,
           c_lam_k2, c_subln_g, final_g):
    batch, seq, d = x.shape
    depth = norm_g.shape[0]
    assert d == D_MODEL and seq % PROJ_TM == 0 and (batch * seq) % OUT_TM == 0
    x2 = x.reshape(batch * seq, d)
    cos, sin_lo, sin_hi = _rope_tables(seq)
    fg = final_g.reshape(1, d)
    row = lambda a: a.reshape(1, -1)
    for layer in range(depth):
        g = row(norm_g[layer])
        final = layer == depth - 1
        if layer % 2 == 0:
            e = layer // 2
            w_in = ev_w_in[e]
            w_main = jnp.concatenate([w_in[:, :GATE_LO], w_in[:, GATE_LO + 2 * B_HEADS:]], axis=1).astype(BF16)
            w_gate = jnp.pad(w_in[:, GATE_LO:GATE_LO + 2 * B_HEADS], ((0, 0), (0, LANES - 2 * B_HEADS))).astype(BF16)
            proj, gates = _even_proj(x2, g, w_main, w_gate)
            gbias = jnp.pad(jnp.concatenate([b_ig_b[e], b_fg_b[e]]), (0, LANES - 2 * B_HEADS)).reshape(1, LANES)
            y = _even_mixer(proj, gates, row(a_ln_g[e]), a_ws[e].astype(BF16), a_bs[e].T, b_conv_w[e],
                            row(b_conv_b[e]), b_wq[e].astype(BF16), b_wk[e].astype(BF16),
                            b_wv[e].astype(BF16), gbias, row(b_gn_g[e]), row(b_skip[e]), batch, seq)
            x2 = _out_proj(y, ev_w_out[e].astype(BF16), x2, fg, final)
        else:
            o = layer // 2
            proj = _odd_proj(x2, g, od_w_in[o].astype(BF16), cos, sin_lo, sin_hi, seq)
            lamp = jnp.stack([c_lam_q1[o], c_lam_k1[o], c_lam_q2[o], c_lam_k2[o]])
            y = _attention(proj, lamp, row(c_subln_g[o]), batch, seq, _lambda_init(layer))
            x2 = _out_proj(y, od_w_out[o].astype(BF16), x2, fg, final)
    return x2.reshape(batch, seq, d)
```

```python
import functools
import math

import jax
import jax.numpy as jnp
from jax import lax
from jax.experimental import pallas as pl
from jax.experimental.pallas import tpu as pltpu

F32 = jnp.float32
BF16 = jnp.bfloat16

D_MODEL = 1024
A_GROUPS = 8
CHUNK = 128
B_HEADS = 4
B_HEAD_DIM = 256
B_CONV = 4
C_HEADS = 16
C_HEAD_DIM = 64
LANES = 128
ROPE_THETA = 10000.0
NORM_EPS = 1e-6
LOG2E = 1.4426950408889634
EVEN_MAIN = 6 * D_MODEL
ODD_IN = 8 * D_MODEL
GATE_LO = 5 * D_MODEL

VMEM_LIMIT_BYTES = 56 * 1024 * 1024


def _params(semantics):
    return pltpu.CompilerParams(dimension_semantics=semantics, vmem_limit_bytes=VMEM_LIMIT_BYTES)


def _rmsnorm_rows(x, g):
    return x * lax.rsqrt(jnp.mean(x * x, axis=-1, keepdims=True) + NORM_EPS) * g


def _layernorm_rows(x, g):
    xc = x - jnp.mean(x, axis=-1, keepdims=True)
    return xc * lax.rsqrt(jnp.mean(xc * xc, axis=-1, keepdims=True) + NORM_EPS) * g


def _dot(a, b):
    return jnp.dot(a, b, preferred_element_type=F32)


def _dot_nt(a, b):
    return lax.dot_general(a, b, (((1,), (1,)), ((), ())), preferred_element_type=F32)


def _dot_tn(a, b):
    return lax.dot_general(a, b, (((0,), (0,)), ((), ())), preferred_element_type=F32)


PROJ_TM = 512
PROJ_TN = 512


def _even_proj_kernel(x_ref, g_ref, w_ref, wg_ref, o_ref, gate_ref):
    hn = _rmsnorm_rows(x_ref[...], g_ref[...]).astype(BF16)
    gate_ref[...] = _dot(hn, wg_ref[...])
    for c in range(EVEN_MAIN // PROJ_TN):
        cols = slice(c * PROJ_TN, (c + 1) * PROJ_TN)
        o_ref[:, cols] = _dot(hn, w_ref[:, cols]).astype(BF16)


def _odd_proj_kernel(x_ref, g_ref, w_ref, cos_ref, sin_lo_ref, sin_hi_ref, o_ref):
    hn = _rmsnorm_rows(x_ref[...], g_ref[...]).astype(BF16)
    cos, sin_lo, sin_hi = cos_ref[...], sin_lo_ref[...], sin_hi_ref[...]
    q_cols = C_HEADS * 2 * C_HEAD_DIM
    for c in range(ODD_IN // PROJ_TN):
        cols = slice(c * PROJ_TN, (c + 1) * PROJ_TN)
        acc = _dot(hn, w_ref[:, cols])
        if c * PROJ_TN < 2 * q_cols:
            scale = C_HEAD_DIM ** -0.5 * LOG2E if c * PROJ_TN < q_cols else 1.0
            for j in range(PROJ_TN // LANES):
                blk = acc[:, j * LANES:(j + 1) * LANES]
                rot = (blk * cos + pltpu.roll(blk, LANES - C_HEAD_DIM // 2, 1) * sin_lo
                       + pltpu.roll(blk, C_HEAD_DIM // 2, 1) * sin_hi)
                o_ref[:, c * PROJ_TN + j * LANES:c * PROJ_TN + (j + 1) * LANES] = (rot * scale).astype(BF16)
        else:
            o_ref[:, cols] = acc.astype(BF16)


def _even_proj(x2, g, w_main, w_gate):
    t = x2.shape[0]
    return pl.pallas_call(
        _even_proj_kernel,
        grid=(t // PROJ_TM,),
        in_specs=[
            pl.BlockSpec((PROJ_TM, D_MODEL), lambda i: (i, 0)),
            pl.BlockSpec((1, D_MODEL), lambda i: (0, 0)),
            pl.BlockSpec((D_MODEL, EVEN_MAIN), lambda i: (0, 0), pipeline_mode=pl.Buffered(1)),
            pl.BlockSpec((D_MODEL, LANES), lambda i: (0, 0)),
        ],
        out_specs=[
            pl.BlockSpec((PROJ_TM, EVEN_MAIN), lambda i: (i, 0)),
            pl.BlockSpec((PROJ_TM, LANES), lambda i: (i, 0)),
        ],
        out_shape=[
            jax.ShapeDtypeStruct((t, EVEN_MAIN), BF16),
            jax.ShapeDtypeStruct((t, LANES), F32),
        ],
        compiler_params=_params(("parallel",)),
        name="even_proj",
    )(x2, g, w_main, w_gate)


def _odd_proj(x2, g, w, cos, sin_lo, sin_hi, seq):
    t = x2.shape[0]
    pos_blocks = seq // PROJ_TM
    tab = pl.BlockSpec((PROJ_TM, LANES), lambda i: (i % pos_blocks, 0))
    return pl.pallas_call(
        _odd_proj_kernel,
        grid=(t // PROJ_TM,),
        in_specs=[
            pl.BlockSpec((PROJ_TM, D_MODEL), lambda i: (i, 0)),
            pl.BlockSpec((1, D_MODEL), lambda i: (0, 0)),
            pl.BlockSpec((D_MODEL, ODD_IN), lambda i: (0, 0), pipeline_mode=pl.Buffered(1)),
            tab, tab, tab,
        ],
        out_specs=pl.BlockSpec((PROJ_TM, ODD_IN), lambda i: (i, 0)),
        out_shape=jax.ShapeDtypeStruct((t, ODD_IN), BF16),
        compiler_params=_params(("parallel",)),
        name="odd_proj",
    )(x2, g, w, cos, sin_lo, sin_hi)


OUT_TM = 512


def _out_proj_kernel(y_ref, w_ref, x_ref, fg_ref, o_ref, *, final):
    acc = _dot(y_ref[...], w_ref[...]) + x_ref[...]
    if final:
        acc = _rmsnorm_rows(acc, fg_ref[...])
    o_ref[...] = acc


def _out_proj(y, w, x2, final_g, final):
    t, k = y.shape
    return pl.pallas_call(
        functools.partial(_out_proj_kernel, final=final),
        grid=(t // OUT_TM,),
        in_specs=[
            pl.BlockSpec((OUT_TM, k), lambda i: (i, 0)),
            pl.BlockSpec((k, D_MODEL), lambda i: (0, 0)),
            pl.BlockSpec((OUT_TM, D_MODEL), lambda i: (i, 0)),
            pl.BlockSpec((1, D_MODEL), lambda i: (0, 0)),
        ],
        out_specs=pl.BlockSpec((OUT_TM, D_MODEL), lambda i: (i, 0)),
        out_shape=jax.ShapeDtypeStruct((t, D_MODEL), F32),
        compiler_params=_params(("parallel",)),
        name="out_proj_final" if final else "out_proj",
    )(y, w, x2, final_g)


MIX_BATCH = 4
MIX_UNITS = 16
GELU_C0 = math.sqrt(2.0 / math.pi)
GELU_C1 = 0.044715 * GELU_C0


def _gelu_tanh(x):
    hx = 0.5 * x
    return hx + hx * jnp.tanh(x * (GELU_C0 + GELU_C1 * (x * x)))


def _silu_tanh(x):
    hx = 0.5 * x
    return hx + hx * jnp.tanh(hx)


def _even_mixer_kernel(u_ref, va_ref, za_ref, xm_ref, og_ref, zb_ref, gate_ref,
                       ln_g_ref, ws_ref, bs_ref, cw_ref, cb_ref, wq_ref, wk_ref, wv_ref,
                       gbias_ref, gn_g_ref, skip_ref,
                       o_ref, prev_ref, c_ref, n_ref, m_ref):
    @pl.when(pl.program_id(1) == 0)
    def _():
        prev_ref[...] = jnp.zeros_like(prev_ref)
        c_ref[...] = jnp.zeros_like(c_ref)
        n_ref[...] = jnp.zeros_like(n_ref)
        m_ref[...] = jnp.zeros_like(m_ref)

    seqs = range(u_ref.shape[0])
    row = lax.broadcasted_iota(jnp.int32, (CHUNK, CHUNK), 0)
    col = lax.broadcasted_iota(jnp.int32, (CHUNK, CHUNK), 1)
    causal = col <= row

    w_s = [jnp.where(causal, ws_ref[g], jnp.zeros((), BF16)) for g in range(A_GROUPS)]
    for b in seqs:
        u = _gelu_tanh(u_ref[b].astype(F32))
        va = _layernorm_rows(_gelu_tanh(va_ref[b].astype(F32)), ln_g_ref[...]).astype(BF16)
        za = za_ref[b].astype(F32)
        for g in range(A_GROUPS):
            cols = slice(g * CHUNK, (g + 1) * CHUNK)
            sg = _dot(w_s[g], va[:, cols]) + bs_ref[:, g:g + 1]
            o_ref[b, :, cols] = (u[:, cols] * sg * _silu_tanh(za[:, cols])).astype(BF16)

    sel_row = lax.broadcasted_iota(jnp.int32, ((B_CONV - 1) * CHUNK, 2 * CHUNK), 0)
    sel_col = lax.broadcasted_iota(jnp.int32, ((B_CONV - 1) * CHUNK, 2 * CHUNK), 1)
    chunk_bits = CHUNK.bit_length() - 1
    shift = jnp.right_shift(sel_row, chunk_bits) + 1
    select = (sel_col == jnp.bitwise_and(sel_row, CHUNK - 1) + CHUNK - shift).astype(BF16)
    tri = causal.astype(BF16)
    xc, xc_b, xm_b, pre, pre_t, bcum, bcum_t = [], [], [], [], [], [], []
    for b in seqs:
        xm_b.append(xm_ref[b])
        window = jnp.concatenate([prev_ref[b], xm_b[b]], axis=0)
        prev_ref[b] = xm_b[b]
        shifted = _dot(select, window)
        conv = xm_b[b].astype(F32) * cw_ref[B_CONV - 1:B_CONV, :] + cb_ref[...]
        for j in range(1, B_CONV):
            conv = conv + shifted[(j - 1) * CHUNK:j * CHUNK, :] * cw_ref[B_CONV - 1 - j:B_CONV - j, :]
        xc.append(_silu_tanh(conv))
        xc_b.append(xc[b].astype(BF16))

        pre.append(gate_ref[b] + gbias_ref[...])
        logf = jnp.minimum(pre[b], 0.0) - jnp.log(1.0 + jnp.exp(-jnp.abs(pre[b])))
        logf_hi = logf.astype(BF16)
        logf_lo = (logf - logf_hi.astype(F32)).astype(BF16)
        bcum.append(_dot(tri, logf_hi) + _dot(tri, logf_lo))
        pre_t.append(pre[b].T)
        bcum_t.append(bcum[b].T)

    all_units = [(b, h) for b in seqs for h in range(B_HEADS)]
    hcols = [slice(h * B_HEAD_DIM, (h + 1) * B_HEAD_DIM) for h in range(B_HEADS)]
    for first in range(0, len(all_units), MIX_UNITS):
        units = all_units[first:first + MIX_UNITS]
        q = [_dot(xc_b[b][:, hcols[h]], wq_ref[h]) for b, h in units]
        k = [_dot(xc_b[b][:, hcols[h]], wk_ref[h]) * (B_HEAD_DIM ** -0.5) for b, h in units]
        v_b = [_dot(xm_b[b][:, hcols[h]], wv_ref[h]).astype(BF16) for b, h in units]
        idx = range(len(units))
        q_b = [q[i].astype(BF16) for i in idx]

        i_col = [pre[b][:, h:h + 1] for b, h in units]
        i_row = [pre_t[b][h:h + 1, :] for b, h in units]
        b_col = [bcum[b][:, B_HEADS + h:B_HEADS + h + 1] for b, h in units]
        b_row = [bcum_t[b][B_HEADS + h:B_HEADS + h + 1, :] for b, h in units]
        b_last = [b_col[i][CHUNK - 1:CHUNK, :] for i in idx]
        m_prev = [m_ref[b, h][:, 0:1] for b, h in units]
        c_prev = [c_ref[b, h] for b, h in units]
        n_prev = [n_ref[b, h] for b, h in units]

        a_col = [b_col[i] + m_prev[i] for i in idx]
        dmat = [jnp.where(causal, b_col[i] - b_row[i] + i_row[i], -jnp.inf) for i in idx]
        m_t = [jnp.maximum(a_col[i], jnp.max(dmat[i], axis=1, keepdims=True)) for i in idx]
        sc = [_dot_nt(q_b[i], k[i].astype(BF16)) * jnp.exp(dmat[i] - m_t[i]) for i in idx]
        w_inter = [jnp.exp(a_col[i] - m_t[i]) for i in idx]
        q_c = [_dot(q_b[i], c_prev[i].astype(BF16)) for i in idx]
        num = [_dot(sc[i].astype(BF16), v_b[i]) + w_inter[i] * q_c[i] for i in idx]
        den = [jnp.sum(sc[i], axis=1, keepdims=True)
               + w_inter[i] * jnp.sum(q[i] * n_prev[i], axis=1, keepdims=True) for i in idx]
        h_t = [num[i] / jnp.maximum(jnp.abs(den[i]), jnp.exp(-m_t[i])) for i in idx]

        g_col = [b_last[i] - b_col[i] + i_col[i] for i in idx]
        g_row = [b_last[i] - b_row[i] + i_row[i] for i in idx]
        m_new = [jnp.maximum(b_last[i] + m_prev[i], jnp.max(g_row[i], axis=1, keepdims=True))
                 for i in idx]
        kw = [k[i] * jnp.exp(g_col[i] - m_new[i]) for i in idx]
        decay = [jnp.exp(b_last[i] + m_prev[i] - m_new[i]) for i in idx]
        for i, (b, h) in enumerate(units):
            c_ref[b, h] = decay[i] * c_prev[i] + _dot_tn(kw[i].astype(BF16), v_b[i])
        for i, (b, h) in enumerate(units):
            n_ref[b, h] = decay[i] * n_prev[i] + jnp.sum(kw[i], axis=0, keepdims=True)
            m_ref[b, h] = jnp.broadcast_to(m_new[i], (1, LANES))

        for i, (b, h) in enumerate(units):
            og = og_ref[b, :, hcols[h]].astype(F32)
            zb = zb_ref[b, :, hcols[h]].astype(F32)
            gated = h_t[i] * (0.5 + 0.5 * jnp.tanh(0.5 * og))
            h_n = _layernorm_rows(gated, gn_g_ref[:, hcols[h]])
            y_b = (h_n + skip_ref[:, hcols[h]] * xc[b][:, hcols[h]]) * _silu_tanh(zb)
            o_ref[b, :, D_MODEL + h * B_HEAD_DIM:D_MODEL + (h + 1) * B_HEAD_DIM] = y_b.astype(BF16)


def _even_mixer(proj, gates, ln_g, ws, bs_t, cw, cb, wq, wk, wv, gbias, gn_g, skip, batch, seq):
    t = proj.shape[0]
    nc = seq // CHUNK
    mb = MIX_BATCH if batch % MIX_BATCH == 0 else 1
    proj3 = proj.reshape(batch, seq, EVEN_MAIN)
    gates3 = gates.reshape(batch, seq, LANES)

    def col_block(j):
        return pl.BlockSpec((mb, CHUNK, D_MODEL), lambda b, c, j=j: (b, c, j))

    def whole(a):
        nd = a.ndim
        return pl.BlockSpec(a.shape, lambda b, c, nd=nd: (0,) * nd)

    consts = (ln_g, ws, bs_t, cw, cb, wq, wk, wv, gbias, gn_g, skip)
    out = pl.pallas_call(
        _even_mixer_kernel,
        grid=(batch // mb, nc),
        in_specs=[col_block(j) for j in range(6)]
        + [pl.BlockSpec((mb, CHUNK, LANES), lambda b, c: (b, c, 0))]
        + [whole(a) for a in consts],
        out_specs=pl.BlockSpec((mb, CHUNK, 2 * D_MODEL), lambda b, c: (b, c, 0)),
        out_shape=jax.ShapeDtypeStruct((batch, seq, 2 * D_MODEL), BF16),
        scratch_shapes=[
            pltpu.VMEM((mb, CHUNK, D_MODEL), BF16),
            pltpu.VMEM((mb, B_HEADS, B_HEAD_DIM, B_HEAD_DIM), F32),
            pltpu.VMEM((mb, B_HEADS, 1, B_HEAD_DIM), F32),
            pltpu.VMEM((mb, B_HEADS, 1, LANES), F32),
        ],
        compiler_params=_params(("parallel", "arbitrary")),
        name="even_mixer",
    )(proj3, proj3, proj3, proj3, proj3, proj3, gates3, *consts)
    return out.reshape(t, 2 * D_MODEL)


ATTN_TILE = 512
ATTN_ROWS = 64
ATTN_BAND = 256
ATTN_OFF_ROWS = 256
ATTN_UNROLL = 12


def _attn_kernel(lamp_ref, subln_ref, q_ref, k_ref, v_ref, z_ref, o_ref,
                 vext_ref, qc_ref, p0_ref, p1_ref, alpha0_ref, alpha1_ref, m_ref, acc_ref,
                 *, tile, lam_init):
    p_refs, alpha_refs = (p0_ref, p1_ref), (alpha0_ref, alpha1_ref)
    seq = q_ref.shape[0]
    nq = seq // tile
    rows = min(ATTN_ROWS, tile)
    half = min(ATTN_BAND, tile)
    off_rows = min(ATTN_OFF_ROWS, tile)

    lane = lax.broadcasted_iota(jnp.int32, (tile, LANES), 1)
    zero = jnp.zeros((), BF16)

    def start(i):
        return pl.multiple_of(jnp.asarray(i, jnp.int32) * tile, tile)

    def prepare_tile(i):
        at = pl.ds(start(i), tile)
        vext_ref[at, :LANES] = v_ref[at, :]
        vext_ref[at, LANES:] = jnp.ones((tile, LANES), BF16)
        q = q_ref[at, :]
        qc_ref[0, at, :] = jnp.where(lane < C_HEAD_DIM, q, zero)
        qc_ref[1, at, :] = jnp.where(lane >= C_HEAD_DIM, q, zero)

    bands = [(r0, r0 + half) for r0 in range(0, tile, half)]

    diag_units = [(c, r0, half, keys) for c in range(2) for r0, keys in bands]
    off_units = [(c, r0, off_rows, tile) for c in range(2) for r0 in range(0, tile, off_rows)]

    def softmax_unit(qi, j, slot, unit, diagonal):
        c, r0, nrows, keys = unit
        k = k_ref[pl.ds(start(j), keys), :]
        s_all = _dot_nt(qc_ref[c, pl.ds(start(qi) + r0, nrows), :], k)
        for r in range(0, nrows, rows):
            dst = slice(r0 + r, r0 + r + rows)
            s = s_all[r:r + rows, :]
            if diagonal:
                keep = (lax.broadcasted_iota(jnp.int32, (rows, keys), 1)
                        <= lax.broadcasted_iota(jnp.int32, (rows, keys), 0) + (r0 + r))
                s = jnp.where(keep, s, -jnp.inf)
                m_new = jnp.broadcast_to(jnp.max(s, axis=1, keepdims=True), (rows, LANES))
            else:
                m_prev = m_ref[qi, c, dst, :]
                m_new = jnp.maximum(m_prev, jnp.max(s, axis=1, keepdims=True))
                alpha_refs[slot][c, dst, :] = jnp.exp2(m_prev - m_new)
            p = jnp.exp2(s - jnp.concatenate([m_new] * (keys // LANES), axis=1))
            p_refs[slot][c, dst, :keys] = p.astype(BF16)
            m_ref[qi, c, dst, :] = m_new

    def values_unit(qi, j, slot, unit, diagonal):
        c, r0, nrows, keys = unit
        dst = slice(r0, r0 + nrows)
        pv = _dot(p_refs[slot][c, dst, :keys], vext_ref[pl.ds(start(j), keys), :])
        if diagonal:
            acc_ref[qi, c, dst, :] = pv
        else:
            alpha = alpha_refs[slot][c, dst, :]
            acc_ref[qi, c, dst, :] = jnp.concatenate([alpha, alpha], axis=1) * acc_ref[qi, c, dst, :] + pv

    def pipeline(blocks, diagonal, unroll, after_tail_step=None):
        nblk = len(blocks)
        units = diag_units if diagonal else off_units

        def step(cur, prev, t, prepare=None):
            for group in ([units] if diagonal else [[unit] for unit in units]):
                for unit in group:
                    if cur is not None:
                        softmax_unit(*cur, t % 2, unit, diagonal)
                for unit in group:
                    if prev is not None:
                        values_unit(*prev, (t - 1) % 2, unit, diagonal)
            if prepare is not None:
                prepare_tile(prepare)

        def static_step(t):
            step(blocks[t] if t < nblk else None, blocks[t - 1] if 1 <= t <= nblk else None, t,
                 t + 1 if diagonal and t + 1 < nblk else None)

        def next_block(qi, j):
            if diagonal:
                return qi + 1, j + 1
            wrap = j + 1 == qi
            return jnp.where(wrap, qi + 1, qi), jnp.where(wrap, 0, j + 1)

        if diagonal:
            prepare_tile(0)
        static_step(0)
        n_trips = max(nblk - (2 if diagonal else 1), 0) // unroll
        if n_trips > 0:
            def body(i, carry):
                prev, cur = carry
                for u in range(unroll):
                    step(cur, prev, 1 + u, cur[0] + 1 if diagonal else None)
                    prev, cur = cur, next_block(*cur)
                return prev, cur

            as_i32 = lambda b: (jnp.int32(b[0]), jnp.int32(b[1]))
            lax.fori_loop(0, n_trips, body, (as_i32(blocks[0]), as_i32(blocks[1])))
        for t in range(1 + unroll * n_trips, nblk + 1):
            static_step(t)
            if after_tail_step is not None:
                after_tail_step(t, unroll * n_trips, nblk)

    lp = lamp_ref[...]
    lam = (jnp.exp(jnp.sum(lp[0:1] * lp[1:2], axis=-1, keepdims=True))
           - jnp.exp(jnp.sum(lp[2:3] * lp[3:4], axis=-1, keepdims=True)) + lam_init)

    subln_gain = subln_ref[...] * (1.0 - lam_init)

    def finalize(qi, carry):
        a0, a1 = acc_ref[qi, 0], acc_ref[qi, 1]
        o = a0[:, :LANES] / a0[:, LANES:] - lam * (a1[:, :LANES] / a1[:, LANES:])
        z = z_ref[pl.ds(start(qi), tile), :].astype(F32)
        o_ref[pl.ds(start(qi), tile), :] = (_rmsnorm_rows(o, subln_gain) * _silu_tanh(z)).astype(BF16)
        return carry

    pipeline([(qi, qi) for qi in range(nq)], True, 2 * nq)
    if nq == 1:
        finalize(0, 0)
        return

    done_at = {0: 0}
    for qi in range(1, nq):
        done_at[qi] = qi * (qi + 1) // 2
    finished = set()

    def finalize_ready(t, loop_steps, nblk):
        full_tail_steps = max(nblk - loop_steps - 1, 1)
        ready = [qi for qi in range(nq) if done_at[qi] <= max(t, loop_steps) and qi not in finished]
        already = [qi for qi in ready if done_at[qi] <= loop_steps]
        share = -(-sum(done_at[qi] <= loop_steps for qi in range(nq)) // full_tail_steps)
        for qi in [qi for qi in ready if done_at[qi] > loop_steps] + already[:share]:
            finalize(qi, 0)
            finished.add(qi)

    pipeline([(qi, j) for qi in range(1, nq) for j in range(qi)], False, ATTN_UNROLL, finalize_ready)
    for qi in range(nq):
        if qi not in finished:
            finalize(qi, 0)


def _attention(proj, lamp, subln, batch, seq, lam_init):
    t = proj.shape[0]
    tile = min(ATTN_TILE, seq)
    nq = seq // tile

    def head_block(part):
        return pl.BlockSpec((seq, LANES), lambda b, h, part=part: (b, part * C_HEADS + h))

    slots = lambda shape, dtype: [pltpu.VMEM(shape, dtype), pltpu.VMEM(shape, dtype)]
    return pl.pallas_call(
        functools.partial(_attn_kernel, tile=tile, lam_init=lam_init),
        grid=(batch, C_HEADS),
        in_specs=[
            pl.BlockSpec(lamp.shape, lambda b, h: (0, 0)),
            pl.BlockSpec(subln.shape, lambda b, h: (0, 0)),
            head_block(0), head_block(1), head_block(2), head_block(3),
        ],
        out_specs=pl.BlockSpec((seq, LANES), lambda b, h: (b, h)),
        out_shape=jax.ShapeDtypeStruct((t, C_HEADS * LANES), BF16),
        scratch_shapes=[pltpu.VMEM((seq, 2 * LANES), BF16),
                        pltpu.VMEM((2, seq, LANES), BF16)]
        + slots((2, tile, tile), BF16)
        + slots((2, tile, LANES), F32)
        + [pltpu.VMEM((nq, 2, tile, LANES), F32),
           pltpu.VMEM((nq, 2, tile, 2 * LANES), F32)],
        compiler_params=_params(("parallel", "parallel")),
        name="diff_attention",
    )(lamp, subln, proj, proj, proj, proj)


def _lambda_init(layer):
    return 0.8 - 0.6 * math.exp(-0.3 * layer)


def _rope_tables(seq):
    half = C_HEAD_DIM // 2
    inv = ROPE_THETA ** (-jnp.arange(0, C_HEAD_DIM, 2, dtype=F32) / C_HEAD_DIM)
    ang = jnp.arange(seq, dtype=F32)[:, None] * inv[None, :]
    cos = jnp.tile(jnp.cos(ang), (1, LANES // half))
    sin = jnp.tile(jnp.sin(ang), (1, LANES // half))
    first_half = (jnp.arange(LANES) % C_HEAD_DIM) < half
    return cos, jnp.where(first_half, -sin, 0.0), jnp.where(first_half, 0.0, sin)


def kernel(x, norm_g, ev_w_in, ev_w_out, a_ln_g, a_ws, a_bs, b_conv_w, b_conv_b, b_wq, b_wk, b_wv,
           b_ig_b, b_fg_b, b_gn_g, b_skip, od_w_in, od_w_out, c_lam_q1, c_lam_k1, c_lam_q2,
           c_lam_k2, c_subln_g, final_g):
    batch, seq, d = x.shape
    depth = norm_g.shape[0]
    assert d == D_MODEL and seq % PROJ_TM == 0 and (batch * seq) % OUT_TM == 0
    x2 = x.reshape(batch * seq, d)
    cos, sin_lo, sin_hi = _rope_tables(seq)
    fg = final_g.reshape(1, d)
    row = lambda a: a.reshape(1, -1)
    for layer in range(depth):
        g = row(norm_g[layer])
        final = layer == depth - 1
        if layer % 2 == 0:
            e = layer // 2
            w_in = ev_w_in[e]
            w_main = jnp.concatenate([w_in[:, :GATE_LO], w_in[:, GATE_LO + 2 * B_HEADS:]], axis=1).astype(BF16)
            w_gate = jnp.pad(w_in[:, GATE_LO:GATE_LO + 2 * B_HEADS], ((0, 0), (0, LANES - 2 * B_HEADS))).astype(BF16)
            proj, gates = _even_proj(x2, g, w_main, w_gate)
            gbias = jnp.pad(jnp.concatenate([b_ig_b[e], b_fg_b[e]]), (0, LANES - 2 * B_HEADS)).reshape(1, LANES)
            y = _even_mixer(proj, gates, row(a_ln_g[e]), a_ws[e].astype(BF16), a_bs[e].T, b_conv_w[e],
                            row(b_conv_b[e]), b_wq[e].astype(BF16), b_wk[e].astype(BF16),
                            b_wv[e].astype(BF16), gbias, row(b_gn_g[e]), row(b_skip[e]), batch, seq)
            x2 = _out_proj(y, ev_w_out[e].astype(BF16), x2, fg, final)
        else:
            o = layer // 2
            proj = _odd_proj(x2, g, od_w_in[o].astype(BF16), cos, sin_lo, sin_hi, seq)
            lamp = jnp.stack([c_lam_q1[o], c_lam_k1[o], c_lam_q2[o], c_lam_k2[o]])
            y = _attention(proj, lamp, row(c_subln_g[o]), batch, seq, _lambda_init(layer))
            x2 = _out_proj(y, od_w_out[o].astype(BF16), x2, fg, final)
    return x2.reshape(batch, seq, d)
```

```python
import functools
import math

import jax
import jax.numpy as jnp
from jax import lax
from jax.experimental import pallas as pl
from jax.experimental.pallas import tpu as pltpu

F32 = jnp.float32
BF16 = jnp.bfloat16

D_MODEL = 1024
A_GROUPS = 8
CHUNK = 128
B_HEADS = 4
B_HEAD_DIM = 256
B_CONV = 4
C_HEADS = 16
C_HEAD_DIM = 64
LANES = 128
ROPE_THETA = 10000.0
NORM_EPS = 1e-6
LOG2E = 1.4426950408889634
EVEN_MAIN = 6 * D_MODEL
ODD_IN = 8 * D_MODEL
GATE_LO = 5 * D_MODEL

VMEM_LIMIT_BYTES = 56 * 1024 * 1024


def _params(semantics):
    return pltpu.CompilerParams(dimension_semantics=semantics, vmem_limit_bytes=VMEM_LIMIT_BYTES)


def _rmsnorm_rows(x, g):
    return x * lax.rsqrt(jnp.mean(x * x, axis=-1, keepdims=True) + NORM_EPS) * g


def _layernorm_rows(x, g):
    xc = x - jnp.mean(x, axis=-1, keepdims=True)
    return xc * lax.rsqrt(jnp.mean(xc * xc, axis=-1, keepdims=True) + NORM_EPS) * g


def _dot(a, b):
    return jnp.dot(a, b, preferred_element_type=F32)


def _dot_nt(a, b):
    return lax.dot_general(a, b, (((1,), (1,)), ((), ())), preferred_element_type=F32)


def _dot_tn(a, b):
    return lax.dot_general(a, b, (((0,), (0,)), ((), ())), preferred_element_type=F32)


PROJ_TM = 512
PROJ_TN = 512


def _even_proj_kernel(x_ref, g_ref, w_ref, wg_ref, o_ref, gate_ref):
    hn = _rmsnorm_rows(x_ref[...], g_ref[...]).astype(BF16)
    gate_ref[...] = _dot(hn, wg_ref[...])
    for c in range(EVEN_MAIN // PROJ_TN):
        cols = slice(c * PROJ_TN, (c + 1) * PROJ_TN)
        o_ref[:, cols] = _dot(hn, w_ref[:, cols]).astype(BF16)


def _odd_proj_kernel(x_ref, g_ref, w_ref, cos_ref, sin_lo_ref, sin_hi_ref, o_ref):
    hn = _rmsnorm_rows(x_ref[...], g_ref[...]).astype(BF16)
    cos, sin_lo, sin_hi = cos_ref[...], sin_lo_ref[...], sin_hi_ref[...]
    q_cols = C_HEADS * 2 * C_HEAD_DIM
    for c in range(ODD_IN // PROJ_TN):
        cols = slice(c * PROJ_TN, (c + 1) * PROJ_TN)
        acc = _dot(hn, w_ref[:, cols])
        if c * PROJ_TN < 2 * q_cols:
            scale = C_HEAD_DIM ** -0.5 * LOG2E if c * PROJ_TN < q_cols else 1.0
            for j in range(PROJ_TN // LANES):
                blk = acc[:, j * LANES:(j + 1) * LANES]
                rot = (blk * cos + pltpu.roll(blk, LANES - C_HEAD_DIM // 2, 1) * sin_lo
                       + pltpu.roll(blk, C_HEAD_DIM // 2, 1) * sin_hi)
                o_ref[:, c * PROJ_TN + j * LANES:c * PROJ_TN + (j + 1) * LANES] = (rot * scale).astype(BF16)
        else:
            o_ref[:, cols] = acc.astype(BF16)


def _even_proj(x2, g, w_main, w_gate):
    t = x2.shape[0]
    return pl.pallas_call(
        _even_proj_kernel,
        grid=(t // PROJ_TM,),
        in_specs=[
            pl.BlockSpec((PROJ_TM, D_MODEL), lambda i: (i, 0)),
            pl.BlockSpec((1, D_MODEL), lambda i: (0, 0)),
            pl.BlockSpec((D_MODEL, EVEN_MAIN), lambda i: (0, 0), pipeline_mode=pl.Buffered(1)),
            pl.BlockSpec((D_MODEL, LANES), lambda i: (0, 0)),
        ],
        out_specs=[
            pl.BlockSpec((PROJ_TM, EVEN_MAIN), lambda i: (i, 0)),
            pl.BlockSpec((PROJ_TM, LANES), lambda i: (i, 0)),
        ],
        out_shape=[
            jax.ShapeDtypeStruct((t, EVEN_MAIN), BF16),
            jax.ShapeDtypeStruct((t, LANES), F32),
        ],
        compiler_params=_params(("parallel",)),
        name="even_proj",
    )(x2, g, w_main, w_gate)


def _odd_proj(x2, g, w, cos, sin_lo, sin_hi, seq):
    t = x2.shape[0]
    pos_blocks = seq // PROJ_TM
    tab = pl.BlockSpec((PROJ_TM, LANES), lambda i: (i % pos_blocks, 0))
    return pl.pallas_call(
        _odd_proj_kernel,
        grid=(t // PROJ_TM,),
        in_specs=[
            pl.BlockSpec((PROJ_TM, D_MODEL), lambda i: (i, 0)),
            pl.BlockSpec((1, D_MODEL), lambda i: (0, 0)),
            pl.BlockSpec((D_MODEL, ODD_IN), lambda i: (0, 0), pipeline_mode=pl.Buffered(1)),
            tab, tab, tab,
        ],
        out_specs=pl.BlockSpec((PROJ_TM, ODD_IN), lambda i: (i, 0)),
        out_shape=jax.ShapeDtypeStruct((t, ODD_IN), BF16),
        compiler_params=_params(("parallel",)),
        name="odd_proj",
    )(x2, g, w, cos, sin_lo, sin_hi)


OUT_TM = 512


def _out_proj_kernel(y_ref, w_ref, x_ref, fg_ref, o_ref, *, final):
    acc = _dot(y_ref[...], w_ref[...]) + x_ref[...]
    if final:
        acc = _rmsnorm_rows(acc, fg_ref[...])
    o_ref[...] = acc


def _out_proj(y, w, x2, final_g, final):
    t, k = y.shape
    return pl.pallas_call(
        functools.partial(_out_proj_kernel, final=final),
        grid=(t // OUT_TM,),
        in_specs=[
            pl.BlockSpec((OUT_TM, k), lambda i: (i, 0)),
            pl.BlockSpec((k, D_MODEL), lambda i: (0, 0)),
            pl.BlockSpec((OUT_TM, D_MODEL), lambda i: (i, 0)),
            pl.BlockSpec((1, D_MODEL), lambda i: (0, 0)),
        ],
        out_specs=pl.BlockSpec((OUT_TM, D_MODEL), lambda i: (i, 0)),
        out_shape=jax.ShapeDtypeStruct((t, D_MODEL), F32),
        compiler_params=_params(("parallel",)),
        name="out_proj_final" if final else "out_proj",
    )(y, w, x2, final_g)


MIX_BATCH = 4
MIX_UNITS = 16
GELU_C0 = math.sqrt(2.0 / math.pi)
GELU_C1 = 0.044715 * GELU_C0


def _gelu_tanh(x):
    hx = 0.5 * x
    return hx + hx * jnp.tanh(x * (GELU_C0 + GELU_C1 * (x * x)))


def _silu_tanh(x):
    hx = 0.5 * x
    return hx + hx * jnp.tanh(hx)


def _even_mixer_kernel(u_ref, va_ref, za_ref, xm_ref, og_ref, zb_ref, gate_ref,
                       ln_g_ref, ws_ref, bs_ref, cw_ref, cb_ref, wq_ref, wk_ref, wv_ref,
                       gbias_ref, gn_g_ref, skip_ref,
                       o_ref, prev_ref, c_ref, n_ref, m_ref):
    @pl.when(pl.program_id(1) == 0)
    def _():
        prev_ref[...] = jnp.zeros_like(prev_ref)
        c_ref[...] = jnp.zeros_like(c_ref)
        n_ref[...] = jnp.zeros_like(n_ref)
        m_ref[...] = jnp.zeros_like(m_ref)

    seqs = range(u_ref.shape[0])
    row = lax.broadcasted_iota(jnp.int32, (CHUNK, CHUNK), 0)
    col = lax.broadcasted_iota(jnp.int32, (CHUNK, CHUNK), 1)
    causal = col <= row

    w_s = [jnp.where(causal, ws_ref[g], jnp.zeros((), BF16)) for g in range(A_GROUPS)]
    for b in seqs:
        u = _gelu_tanh(u_ref[b].astype(F32))
        va = _layernorm_rows(_gelu_tanh(va_ref[b].astype(F32)), ln_g_ref[...]).astype(BF16)
        za = za_ref[b].astype(F32)
        for g in range(A_GROUPS):
            cols = slice(g * CHUNK, (g + 1) * CHUNK)
            sg = _dot(w_s[g], va[:, cols]) + bs_ref[:, g:g + 1]
            o_ref[b, :, cols] = (u[:, cols] * sg * _silu_tanh(za[:, cols])).astype(BF16)

    sel_row = lax.broadcasted_iota(jnp.int32, ((B_CONV - 1) * CHUNK, 2 * CHUNK), 0)
    sel_col = lax.broadcasted_iota(jnp.int32, ((B_CONV - 1) * CHUNK, 2 * CHUNK), 1)
    chunk_bits = CHUNK.bit_length() - 1
    shift = jnp.right_shift(sel_row, chunk_bits) + 1
    select = (sel_col == jnp.bitwise_and(sel_row, CHUNK - 1) + CHUNK - shift).astype(BF16)
    tri = causal.astype(BF16)
    xc, xc_b, xm_b, pre, pre_t, bcum, bcum_t = [], [], [], [], [], [], []
    for b in seqs:
        xm_b.append(xm_ref[b])
        window = jnp.concatenate([prev_ref[b], xm_b[b]], axis=0)
        prev_ref[b] = xm_b[b]
        shifted = _dot(select, window)
        conv = xm_b[b].astype(F32) * cw_ref[B_CONV - 1:B_CONV, :] + cb_ref[...]
        for j in range(1, B_CONV):
            conv = conv + shifted[(j - 1) * CHUNK:j * CHUNK, :] * cw_ref[B_CONV - 1 - j:B_CONV - j, :]
        xc.append(_silu_tanh(conv))
        xc_b.append(xc[b].astype(BF16))

        pre.append(gate_ref[b] + gbias_ref[...])
        logf = jnp.minimum(pre[b], 0.0) - jnp.log(1.0 + jnp.exp(-jnp.abs(pre[b])))
        logf_hi = logf.astype(BF16)
        logf_lo = (logf - logf_hi.astype(F32)).astype(BF16)
        bcum.append(_dot(tri, logf_hi) + _dot(tri, logf_lo))
        pre_t.append(pre[b].T)
        bcum_t.append(bcum[b].T)

    all_units = [(b, h) for b in seqs for h in range(B_HEADS)]
    hcols = [slice(h * B_HEAD_DIM, (h + 1) * B_HEAD_DIM) for h in range(B_HEADS)]
    for first in range(0, len(all_units), MIX_UNITS):
        units = all_units[first:first + MIX_UNITS]
        q = [_dot(xc_b[b][:, hcols[h]], wq_ref[h]) for b, h in units]
        k = [_dot(xc_b[b][:, hcols[h]], wk_ref[h]) * (B_HEAD_DIM ** -0.5) for b, h in units]
        v_b = [_dot(xm_b[b][:, hcols[h]], wv_ref[h]).astype(BF16) for b, h in units]
        idx = range(len(units))
        q_b = [q[i].astype(BF16) for i in idx]

        i_col = [pre[b][:, h:h + 1] for b, h in units]
        i_row = [pre_t[b][h:h + 1, :] for b, h in units]
        b_col = [bcum[b][:, B_HEADS + h:B_HEADS + h + 1] for b, h in units]
        b_row = [bcum_t[b][B_HEADS + h:B_HEADS + h + 1, :] for b, h in units]
        b_last = [b_col[i][CHUNK - 1:CHUNK, :] for i in idx]
        m_prev = [m_ref[b, h][:, 0:1] for b, h in units]
        c_prev = [c_ref[b, h] for b, h in units]
        n_prev = [n_ref[b, h] for b, h in units]

        a_col = [b_col[i] + m_prev[i] for i in idx]
        dmat = [jnp.where(causal, b_col[i] - b_row[i] + i_row[i], -jnp.inf) for i in idx]
        m_t = [jnp.maximum(a_col[i], jnp.max(dmat[i], axis=1, keepdims=True)) for i in idx]
        sc = [_dot_nt(q_b[i], k[i].astype(BF16)) * jnp.exp(dmat[i] - m_t[i]) for i in idx]
        w_inter = [jnp.exp(a_col[i] - m_t[i]) for i in idx]
        q_c = [_dot(q_b[i], c_prev[i].astype(BF16)) for i in idx]
        num = [_dot(sc[i].astype(BF16), v_b[i]) + w_inter[i] * q_c[i] for i in idx]
        den = [jnp.sum(sc[i], axis=1, keepdims=True)
               + w_inter[i] * jnp.sum(q[i] * n_prev[i], axis=1, keepdims=True) for i in idx]
        h_t = [num[i] / jnp.maximum(jnp.abs(den[i]), jnp.exp(-m_t[i])) for i in idx]

        g_col = [b_last[i] - b_col[i] + i_col[i] for i in idx]
        g_row = [b_last[i] - b_row[i] + i_row[i] for i in idx]
        m_new = [jnp.maximum(b_last[i] + m_prev[i], jnp.max(g_row[i], axis=1, keepdims=True))
                 for i in idx]
        kw = [k[i] * jnp.exp(g_col[i] - m_new[i]) for i in idx]
        decay = [jnp.exp(b_last[i] + m_prev[i] - m_new[i]) for i in idx]
        for i, (b, h) in enumerate(units):
            c_ref[b, h] = decay[i] * c_prev[i] + _dot_tn(kw[i].astype(BF16), v_b[i])
        for i, (b, h) in enumerate(units):
            n_ref[b, h] = decay[i] * n_prev[i] + jnp.sum(kw[i], axis=0, keepdims=True)
            m_ref[b, h] = jnp.broadcast_to(m_new[i], (1, LANES))

        for i, (b, h) in enumerate(units):
            og = og_ref[b, :, hcols[h]].astype(F32)
            zb = zb_ref[b, :, hcols[h]].astype(F32)
            gated = h_t[i] * (0.5 + 0.5 * jnp.tanh(0.5 * og))
            h_n = _layernorm_rows(gated, gn_g_ref[:, hcols[h]])
            y_b = (h_n + skip_ref[:, hcols[h]] * xc[b][:, hcols[h]]) * _silu_tanh(zb)
            o_ref[b, :, D_MODEL + h * B_HEAD_DIM:D_MODEL + (h + 1) * B_HEAD_DIM] = y_b.astype(BF16)


def _even_mixer(proj, gates, ln_g, ws, bs_t, cw, cb, wq, wk, wv, gbias, gn_g, skip, batch, seq):
    t = proj.shape[0]
    nc = seq // CHUNK
    mb = MIX_BATCH if batch % MIX_BATCH == 0 else 1
    proj3 = proj.reshape(batch, seq, EVEN_MAIN)
    gates3 = gates.reshape(batch, seq, LANES)

    def col_block(j):
        return pl.BlockSpec((mb, CHUNK, D_MODEL), lambda b, c, j=j: (b, c, j))

    def whole(a):
        nd = a.ndim
        return pl.BlockSpec(a.shape, lambda b, c, nd=nd: (0,) * nd)

    consts = (ln_g, ws, bs_t, cw, cb, wq, wk, wv, gbias, gn_g, skip)
    out = pl.pallas_call(
        _even_mixer_kernel,
        grid=(batch // mb, nc),
        in_specs=[col_block(j) for j in range(6)]
        + [pl.BlockSpec((mb, CHUNK, LANES), lambda b, c: (b, c, 0))]
        + [whole(a) for a in consts],
        out_specs=pl.BlockSpec((mb, CHUNK, 2 * D_MODEL), lambda b, c: (b, c, 0)),
        out_shape=jax.ShapeDtypeStruct((batch, seq, 2 * D_MODEL), BF16),
        scratch_shapes=[
            pltpu.VMEM((mb, CHUNK, D_MODEL), BF16),
            pltpu.VMEM((mb, B_HEADS, B_HEAD_DIM, B_HEAD_DIM), F32),
            pltpu.VMEM((mb, B_HEADS, 1, B_HEAD_DIM), F32),
            pltpu.VMEM((mb, B_HEADS, 1, LANES), F32),
        ],
        compiler_params=_params(("parallel", "arbitrary")),
        name="even_mixer",
    )(proj3, proj3, proj3, proj3, proj3, proj3, gates3, *consts)
    return out.reshape(t, 2 * D_MODEL)


ATTN_TILE = 512
ATTN_ROWS = 64
ATTN_BAND = 256
ATTN_OFF_ROWS = 256
ATTN_UNROLL = 12


def _attn_kernel(lamp_ref, subln_ref, q_ref, k_ref, v_ref, z_ref, o_ref,
                 vext_ref, qc_ref, s0_ref, s1_ref, p0_ref, p1_ref, alpha0_ref, alpha1_ref, m_ref, acc_ref,
                 *, tile, lam_init):
    s_refs, p_refs, alpha_refs = (s0_ref, s1_ref), (p0_ref, p1_ref), (alpha0_ref, alpha1_ref)
    seq = q_ref.shape[0]
    nq = seq // tile
    rows = min(ATTN_ROWS, tile)
    half = min(ATTN_BAND, tile)
    off_rows = min(ATTN_OFF_ROWS, tile)

    lane = lax.broadcasted_iota(jnp.int32, (tile, LANES), 1)
    zero = jnp.zeros((), BF16)

    def start(i):
        return pl.multiple_of(jnp.asarray(i, jnp.int32) * tile, tile)

    def prepare_tile(i):
        at = pl.ds(start(i), tile)
        vext_ref[at, :LANES] = v_ref[at, :]
        vext_ref[at, LANES:] = jnp.ones((tile, LANES), BF16)
        q = q_ref[at, :]
        qc_ref[0, at, :] = jnp.where(lane < C_HEAD_DIM, q, zero)
        qc_ref[1, at, :] = jnp.where(lane >= C_HEAD_DIM, q, zero)

    bands = [(r0, r0 + half) for r0 in range(0, tile, half)]

    diag_units = [(c, r0, half, keys) for c in range(2) for r0, keys in bands]
    off_units = [(c, r0, off_rows, tile) for c in range(2) for r0 in range(0, tile, off_rows)]

    def scores_unit(qi, j, slot, unit):
        c, r0, nrows, keys = unit
        k = k_ref[pl.ds(start(j), keys), :]
        s_refs[slot][c, r0:r0 + nrows, :] = _dot_nt(qc_ref[c, pl.ds(start(qi) + r0, nrows), :], k)

    def softmax_unit(qi, j, slot, unit, diagonal, from_scratch=False):
        c, r0, nrows, keys = unit
        if from_scratch:
            s_all = s_refs[slot][c, r0:r0 + nrows, :]
        else:
            k = k_ref[pl.ds(start(j), keys), :]
            s_all = _dot_nt(qc_ref[c, pl.ds(start(qi) + r0, nrows), :], k)
        for r in range(0, nrows, rows):
            dst = slice(r0 + r, r0 + r + rows)
            s = s_all[r:r + rows, :]
            if diagonal:
                keep = (lax.broadcasted_iota(jnp.int32, (rows, keys), 1)
                        <= lax.broadcasted_iota(jnp.int32, (rows, keys), 0) + (r0 + r))
                s = jnp.where(keep, s, -jnp.inf)
                m_new = jnp.broadcast_to(jnp.max(s, axis=1, keepdims=True), (rows, LANES))
            else:
                m_prev = m_ref[qi, c, dst, :]
                m_new = jnp.maximum(m_prev, jnp.max(s, axis=1, keepdims=True))
                alpha_refs[slot][c, dst, :] = jnp.exp2(m_prev - m_new)
            p = jnp.exp2(s - jnp.concatenate([m_new] * (keys // LANES), axis=1))
            p_refs[slot][c, dst, :keys] = p.astype(BF16)
            m_ref[qi, c, dst, :] = m_new

    def values_unit(qi, j, slot, unit, diagonal):
        c, r0, nrows, keys = unit
        dst = slice(r0, r0 + nrows)
        pv = _dot(p_refs[slot][c, dst, :keys], vext_ref[pl.ds(start(j), keys), :])
        if diagonal:
            acc_ref[qi, c, dst, :] = pv
        else:
            alpha = alpha_refs[slot][c, dst, :]
            acc_ref[qi, c, dst, :] = jnp.concatenate([alpha, alpha], axis=1) * acc_ref[qi, c, dst, :] + pv

    def pipeline(blocks, diagonal, unroll, after_tail_step=None):
        nblk = len(blocks)
        units = diag_units if diagonal else off_units

        def step(cur, prev, t, prepare=None):
            for group in ([units] if diagonal else [[unit] for unit in units]):
                for unit in group:
                    if cur is not None:
                        softmax_unit(*cur, t % 2, unit, diagonal)
                for unit in group:
                    if prev is not None:
                        values_unit(*prev, (t - 1) % 2, unit, diagonal)
            if prepare is not None:
                prepare_tile(prepare)

        def static_step(t):
            step(blocks[t] if t < nblk else None, blocks[t - 1] if 1 <= t <= nblk else None, t,
                 t + 1 if diagonal and t + 1 < nblk else None)

        def next_block(qi, j):
            if diagonal:
                return qi + 1, j + 1
            wrap = j + 1 == qi
            return jnp.where(wrap, qi + 1, qi), jnp.where(wrap, 0, j + 1)

        if diagonal:
            prepare_tile(0)
        static_step(0)
        n_trips = max(nblk - (2 if diagonal else 1), 0) // unroll
        if n_trips > 0:
            def body(i, carry):
                prev, cur = carry
                for u in range(unroll):
                    step(cur, prev, 1 + u, cur[0] + 1 if diagonal else None)
                    prev, cur = cur, next_block(*cur)
                return prev, cur

            as_i32 = lambda b: (jnp.int32(b[0]), jnp.int32(b[1]))
            lax.fori_loop(0, n_trips, body, (as_i32(blocks[0]), as_i32(blocks[1])))
        for t in range(1 + unroll * n_trips, nblk + 1):
            static_step(t)
            if after_tail_step is not None:
                after_tail_step(t, unroll * n_trips, nblk)

    def pipeline3(blocks, unroll, after_tail_step):
        nblk = len(blocks)

        def step(b0, b1, b2, t):
            for unit in off_units:
                if b0 is not None:
                    scores_unit(*b0, t % 2, unit)
                if b1 is not None:
                    softmax_unit(*b1, (t - 1) % 2, unit, False, from_scratch=True)
                if b2 is not None:
                    values_unit(*b2, t % 2, unit, False)

        def static_step(t):
            pick = lambda i: blocks[i] if 0 <= i < nblk else None
            step(pick(t), pick(t - 1), pick(t - 2), t)

        def next_block(qi, j):
            wrap = j + 1 == qi
            return jnp.where(wrap, qi + 1, qi), jnp.where(wrap, 0, j + 1)

        static_step(0)
        static_step(1)
        n_trips = max(nblk - 2, 0) // unroll
        if n_trips > 0:
            def body(i, carry):
                b2, b1, b0 = carry
                for u in range(unroll):
                    step(b0, b1, b2, u)
                    b2, b1, b0 = b1, b0, next_block(*b0)
                return b2, b1, b0

            as_i32 = lambda b: (jnp.int32(b[0]), jnp.int32(b[1]))
            lax.fori_loop(0, n_trips, body, tuple(as_i32(b) for b in blocks[:3]))
        for t in range(2 + unroll * n_trips, nblk + 2):
            static_step(t)
            after_tail_step(t, 1 + unroll * n_trips, nblk + 1)

    lp = lamp_ref[...]
    lam = (jnp.exp(jnp.sum(lp[0:1] * lp[1:2], axis=-1, keepdims=True))
           - jnp.exp(jnp.sum(lp[2:3] * lp[3:4], axis=-1, keepdims=True)) + lam_init)

    subln_gain = subln_ref[...] * (1.0 - lam_init)

    def finalize(qi, carry):
        a0, a1 = acc_ref[qi, 0], acc_ref[qi, 1]
        o = a0[:, :LANES] / a0[:, LANES:] - lam * (a1[:, :LANES] / a1[:, LANES:])
        z = z_ref[pl.ds(start(qi), tile), :].astype(F32)
        o_ref[pl.ds(start(qi), tile), :] = (_rmsnorm_rows(o, subln_gain) * _silu_tanh(z)).astype(BF16)
        return carry

    pipeline([(qi, qi) for qi in range(nq)], True, 2 * nq)
    if nq == 1:
        finalize(0, 0)
        return

    done_at = {0: 0}
    for qi in range(1, nq):
        done_at[qi] = qi * (qi + 1) // 2 + 1
    finished = set()

    def finalize_ready(t, loop_steps, nblk):
        full_tail_steps = max(nblk - loop_steps - 1, 1)
        ready = [qi for qi in range(nq) if done_at[qi] <= max(t, loop_steps) and qi not in finished]
        already = [qi for qi in ready if done_at[qi] <= loop_steps]
        share = -(-sum(done_at[qi] <= loop_steps for qi in range(nq)) // full_tail_steps)
        for qi in [qi for qi in ready if done_at[qi] > loop_steps] + already[:share]:
            finalize(qi, 0)
            finished.add(qi)

    pipeline3([(qi, j) for qi in range(1, nq) for j in range(qi)], ATTN_UNROLL, finalize_ready)
    for qi in range(nq):
        if qi not in finished:
            finalize(qi, 0)


def _attention(proj, lamp, subln, batch, seq, lam_init):
    t = proj.shape[0]
    tile = min(ATTN_TILE, seq)
    nq = seq // tile

    def head_block(part):
        return pl.BlockSpec((seq, LANES), lambda b, h, part=part: (b, part * C_HEADS + h))

    slots = lambda shape, dtype: [pltpu.VMEM(shape, dtype), pltpu.VMEM(shape, dtype)]
    return pl.pallas_call(
        functools.partial(_attn_kernel, tile=tile, lam_init=lam_init),
        grid=(batch, C_HEADS),
        in_specs=[
            pl.BlockSpec(lamp.shape, lambda b, h: (0, 0)),
            pl.BlockSpec(subln.shape, lambda b, h: (0, 0)),
            head_block(0), head_block(1), head_block(2), head_block(3),
        ],
        out_specs=pl.BlockSpec((seq, LANES), lambda b, h: (b, h)),
        out_shape=jax.ShapeDtypeStruct((t, C_HEADS * LANES), BF16),
        scratch_shapes=[pltpu.VMEM((seq, 2 * LANES), BF16),
                        pltpu.VMEM((2, seq, LANES), BF16)]
        + slots((2, tile, tile), F32)
        + slots((2, tile, tile), BF16)
        + slots((2, tile, LANES), F32)
        + [pltpu.VMEM((nq, 2, tile, LANES), F32),
           pltpu.VMEM((nq, 2, tile, 2 * LANES), F32)],
        compiler_params=_params(("parallel", "parallel")),
        name="diff_attention",
    )(lamp, subln, proj, proj, proj, proj)


def _lambda_init(layer):
    return 0.8 - 0.6 * math.exp(-0.3 * layer)


def _rope_tables(seq):
    half = C_HEAD_DIM // 2
    inv = ROPE_THETA ** (-jnp.arange(0, C_HEAD_DIM, 2, dtype=F32) / C_HEAD_DIM)
    ang = jnp.arange(seq, dtype=F32)[:, None] * inv[None, :]
    cos = jnp.tile(jnp.cos(ang), (1, LANES // half))
    sin = jnp.tile(jnp.sin(ang), (1, LANES // half))
    first_half = (jnp.arange(LANES) % C_HEAD_DIM) < half
    return cos, jnp.where(first_half, -sin, 0.0), jnp.where(first_half, 0.0, sin)


def kernel(x, norm_g, ev_w_in, ev_w_out, a_ln_g, a_ws, a_bs, b_conv_w, b_conv_b, b_wq, b_wk, b_wv,
           b_ig_b, b_fg_b, b_gn_g, b_skip, od_w_in, od_w_out, c_lam_q1, c_lam_k1, c_lam_q2,
           c_lam_k2, c_subln_g, final_g):
    batch, seq, d = x.shape
    depth = norm_g.shape[0]
    assert d == D_MODEL and seq % PROJ_TM == 0 and (batch * seq) % OUT_TM == 0
    x2 = x.reshape(batch * seq, d)
    cos, sin_lo, sin_hi = _rope_tables(seq)
    fg = final_g.reshape(1, d)
    row = lambda a: a.reshape(1, -1)
    for layer in range(depth):
        g = row(norm_g[layer])
        final = layer == depth - 1
        if layer % 2 == 0:
            e = layer // 2
            w_in = ev_w_in[e]
            w_main = jnp.concatenate([w_in[:, :GATE_LO], w_in[:, GATE_LO + 2 * B_HEADS:]], axis=1).astype(BF16)
            w_gate = jnp.pad(w_in[:, GATE_LO:GATE_LO + 2 * B_HEADS], ((0, 0), (0, LANES - 2 * B_HEADS))).astype(BF16)
            proj, gates = _even_proj(x2, g, w_main, w_gate)
            gbias = jnp.pad(jnp.concatenate([b_ig_b[e], b_fg_b[e]]), (0, LANES - 2 * B_HEADS)).reshape(1, LANES)
            y = _even_mixer(proj, gates, row(a_ln_g[e]), a_ws[e].astype(BF16), a_bs[e].T, b_conv_w[e],
                            row(b_conv_b[e]), b_wq[e].astype(BF16), b_wk[e].astype(BF16),
                            b_wv[e].astype(BF16), gbias, row(b_gn_g[e]), row(b_skip[e]), batch, seq)
            x2 = _out_proj(y, ev_w_out[e].astype(BF16), x2, fg, final)
        else:
            o = layer // 2
            proj = _odd_proj(x2, g, od_w_in[o].astype(BF16), cos, sin_lo, sin_hi, seq)
            lamp = jnp.stack([c_lam_q1[o], c_lam_k1[o], c_lam_q2[o], c_lam_k2[o]])
            y = _attention(proj, lamp, row(c_subln_g[o]), batch, seq, _lambda_init(layer))
            x2 = _out_proj(y, od_w_out[o].astype(BF16), x2, fg, final)
    return x2.reshape(batch, seq, d)
```
